```python
import math
import jax, jax.numpy as jnp
from jax import lax
import numpy as np

D_MODEL = 2048
BATCH = 32
SEQ = 256
DEPTH = 4
DEC_BATCH = 2
DEC_SEQ = 1024
PAST_LEN = 512

GRID_W = 64
N_EVEN = (DEPTH + 1) // 2
N_ODD = DEPTH // 2
H_A = D_MODEL // 256
DA = 64
DV_A = 2 * DA
H_B = D_MODEL // 256
DK_B = 128
DV_B = 128
H_C = D_MODEL // 128
DK_C = 128
DV_C = 128
D_C = H_C * DV_C
CONV_K = 5
CHUNK = 64
Q_BLOCK = 128
ROPE_BASE = 10000.0
N_GROUPS = 4
E_PER_GROUP = 4
N_EXPERTS = N_GROUPS * E_PER_GROUP
TOP_K_INNER = 2
EXPERT_FF = D_MODEL // 4
EVEN_SPLITS = (H_A * 2 * DA, H_A * 2 * DA, H_A * DV_A, H_B * DK_B, H_B * DK_B, H_B * DV_B, H_B * DV_B)
ODD_SPLITS = (3 * D_C, D_C, 2 * H_C, 2 * H_C)
EPS = 1e-6

kernel_name = 'hybrid_diffusion_prefix_trunk_step'


def rms_norm(x, gain):
    xf = x.astype(jnp.float32)
    y = xf * lax.rsqrt(jnp.mean(xf * xf, axis=-1, keepdims=True) + EPS)
    return (y * gain.astype(jnp.float32)).astype(x.dtype)


def l2_normalize(x):
    xf = x.astype(jnp.float32)
    return xf * lax.rsqrt(jnp.sum(xf * xf, axis=-1, keepdims=True) + EPS)


def split_cols(x, sizes):
    return jnp.split(x, np.cumsum(sizes)[:-1].tolist(), axis=-1)


def adaln_params(cond, w_mod, b_mod):
    m = jax.nn.silu(cond) @ w_mod + b_mod
    return [t[:, None, :] for t in jnp.split(m, 6, axis=-1)]


def axial_rope(n_tokens):
    n_rows = n_tokens // GRID_W
    row = jnp.repeat(jnp.arange(n_rows), GRID_W).astype(jnp.float32)
    col = (jnp.arange(n_rows * GRID_W) % GRID_W).astype(jnp.float32)
    n_freq = DA // 4
    inv_freq = ROPE_BASE ** (-jnp.arange(n_freq, dtype=jnp.float32) / n_freq)
    ang = jnp.concatenate([row[:, None] * inv_freq, col[:, None] * inv_freq], axis=-1)
    return jnp.cos(ang), jnp.sin(ang)


def apply_rope(x, cos, sin):
    cos = cos[:, None, None, :]
    sin = sin[:, None, None, :]
    x1 = x[..., :DA // 2].astype(jnp.float32)
    x2 = x[..., DA // 2:].astype(jnp.float32)
    return jnp.concatenate([x1 * cos - x2 * sin, x1 * sin + x2 * cos], axis=-1).astype(x.dtype)


def diff_attention(q, k, v, lam):
    b, h, lq = q.shape[:3]
    nb = lq // Q_BLOCK
    qb = q.astype(jnp.float32).reshape(b, h, nb, Q_BLOCK, 2, DA).transpose(2, 0, 1, 3, 4, 5)
    kf = k.astype(jnp.float32)
    vf = v.astype(jnp.float32)

    def one_block(qi):
        s = jnp.einsum('bhqcd,bhkcd->bhcqk', qi, kf) * (DA ** -0.5)
        pr = jax.nn.softmax(s, axis=-1)
        a = pr[:, :, 0] - lam * pr[:, :, 1]
        return jnp.einsum('bhqk,bhkd->bhqd', a, vf)

    o = lax.map(one_block, qb)
    return o.transpose(1, 2, 0, 3, 4).reshape(b, h, lq, DV_A).astype(v.dtype)


def retention_scan(q, k, v, log_gamma, s0):
    b, n_tok, h, _ = q.shape
    dv = v.shape[-1]
    n_chunks = n_tok // CHUNK

    def to_chunks(t):
        return t.astype(jnp.float32).reshape(b, n_chunks, CHUNK, h, -1).transpose(1, 0, 3, 2, 4)

    qc, kc, vc = to_chunks(q), to_chunks(k), to_chunks(v)
    idx = jnp.arange(CHUNK, dtype=jnp.float32)
    diff = idx[:, None] - idx[None, :]
    causal = diff >= 0
    lg = log_gamma.astype(jnp.float32)
    decay_in = jnp.where(causal, jnp.exp(jnp.where(causal, diff, 0.0) * lg[:, None, None]), 0.0)
    q_decay = jnp.exp((idx + 1.0)[None, :] * lg[:, None])[..., None]
    k_decay = jnp.exp((CHUNK - 1.0 - idx)[None, :] * lg[:, None])[..., None]
    chunk_decay = jnp.exp(CHUNK * lg)[:, None, None]

    def step(s, inp):
        qi, ki, vi = inp
        scores = jnp.einsum('bhqd,bhkd->bhqk', qi, ki) * decay_in
        o = jnp.einsum('bhqk,bhkv->bhqv', scores, vi) + jnp.einsum('bhqd,bhdv->bhqv', qi * q_decay, s)
        s = s * chunk_decay + jnp.einsum('bhkd,bhkv->bhdv', ki * k_decay, vi)
        return s, o

    s_final, o = lax.scan(step, s0.astype(jnp.float32), (qc, kc, vc))
    return o.transpose(1, 0, 3, 2, 4).reshape(b, n_tok, h, dv), s_final


def gated_delta_chunked(q, k, v, g, beta, s0):
    b, n_tok, h, _ = q.shape
    dv = v.shape[-1]
    n_chunks = n_tok // CHUNK

    def to_chunks(t):
        return t.astype(jnp.float32).reshape(b, n_chunks, CHUNK, h, -1).transpose(1, 0, 3, 2, 4)

    qc, kc, vc = to_chunks(q), to_chunks(k), to_chunks(v)
    gc = to_chunks(g[..., None])[..., 0]
    bc = to_chunks(beta[..., None])[..., 0]
    gcum = jnp.cumsum(gc, axis=-1)
    idx = jnp.arange(CHUNK)
    lower = idx[:, None] >= idx[None, :]
    strict = idx[:, None] > idx[None, :]

    def step(s, inp):
        qi, ki, vi, gi, bi = inp
        dg = gi[..., :, None] - gi[..., None, :]
        decay = jnp.where(lower, jnp.exp(jnp.where(lower, dg, 0.0)), 0.0)
        kk = jnp.einsum('bhid,bhjd->bhij', ki, ki)
        l_mat = jnp.where(strict, bi[..., :, None] * kk * decay, 0.0)
        rhs = jnp.concatenate([vi * bi[..., None], ki * (bi * jnp.exp(gi))[..., None]], axis=-1)
        sol = lax.linalg.triangular_solve(l_mat, rhs, left_side=True, lower=True, unit_diagonal=True)
        u, w = sol[..., :dv], sol[..., dv:]
        v_new = u - jnp.einsum('bhck,bhkv->bhcv', w, s)
        qk = jnp.where(lower, jnp.einsum('bhid,bhjd->bhij', qi, ki) * decay, 0.0)
        o = (jnp.einsum('bhck,bhkv->bhcv', qi * jnp.exp(gi)[..., None], s)
             + jnp.einsum('bhij,bhjv->bhiv', qk, v_new))
        g_last = gi[..., -1:]
        s = (s * jnp.exp(g_last)[..., None]
             + jnp.einsum('bhck,bhcv->bhkv', ki * jnp.exp(g_last - gi)[..., None], v_new))
        return s, o

    s_final, o = lax.scan(step, s0.astype(jnp.float32), (qc, kc, vc, gcum, bc))
    return o.transpose(1, 0, 3, 2, 4).reshape(b, n_tok, h, dv), s_final


def depthwise_conv_centred(x, w):
    k, ch = w.shape
    pad = (k - 1) // 2
    return lax.conv_general_dilated(x, w[:, None, :].astype(x.dtype), window_strides=(1,),
                                    padding=[(pad, k - 1 - pad)],
                                    dimension_numbers=('NWC', 'WIO', 'NWC'),
                                    feature_group_count=ch)


def even_mixer(h, l, p, ctx):
    i = l // 2
    b, n_tok, _ = h.shape
    is_context = ctx is None
    qa, ka, va, qb, kb, vb, gb = split_cols(h @ p['w_in_even'][i], EVEN_SPLITS)
    qa = qa.reshape(b, n_tok, H_A, 2, DA)
    ka = ka.reshape(b, n_tok, H_A, 2, DA)
    va = va.reshape(b, n_tok, H_A, DV_A)
    if not is_context:
        cos, sin = axial_rope(n_tok)
        qa = apply_rope(qa, cos, sin)
        ka = apply_rope(ka, cos, sin)
    qa_t = qa.transpose(0, 2, 1, 3, 4)
    ka_t = ka.transpose(0, 2, 1, 3, 4)
    va_t = va.transpose(0, 2, 1, 3)
    if is_context:
        k_all, v_all = ka_t, va_t
        s0 = jnp.zeros((b, 2, H_B, DK_B, DV_B), jnp.float32)
    else:
        ck, cv, s0 = ctx
        ck = ck.reshape(b, H_A, -1, 2, DA).astype(ka_t.dtype)
        k_all = jnp.concatenate([ck, ka_t], axis=2)
        v_all = jnp.concatenate([cv.astype(va_t.dtype), va_t], axis=2)
    lam_init = 0.8 - 0.6 * math.exp(-0.3 * l)
    dl = p['diff_lambda'][i].astype(jnp.float32)
    lam = jnp.exp(jnp.sum(dl[0] * dl[1])) - jnp.exp(jnp.sum(dl[2] * dl[3])) + lam_init
    o_a = diff_attention(qa_t, k_all, v_all, lam)
    o_a = rms_norm(o_a, p['subln_gain'][i]) * (1.0 - lam_init)
    o_a = o_a.transpose(0, 2, 1, 3).reshape(b, n_tok, H_A * DV_A)
    qb = qb.reshape(b, n_tok, H_B, DK_B)
    kb = kb.reshape(b, n_tok, H_B, DK_B) * (DK_B ** -0.5)
    vb = vb.reshape(b, n_tok, H_B, DV_B)
    log_g = jax.nn.log_sigmoid(p['ret_decay'][i].astype(jnp.float32))
    o_f, s_f = retention_scan(qb, kb, vb, log_g[0], s0[:, 0])
    o_b, s_b = retention_scan(qb[:, ::-1], kb[:, ::-1], vb[:, ::-1], log_g[1], s0[:, 1])
    o_r = (o_f + o_b[:, ::-1]).astype(h.dtype)
    o_r = rms_norm(o_r, p['ret_norm'][i]) * jax.nn.silu(gb.reshape(b, n_tok, H_B, DV_B))
    o_r = o_r.reshape(b, n_tok, H_B * DV_B)
    out = jnp.concatenate([o_a, o_r], axis=-1) @ p['w_out_even'][i]
    if is_context:
        new = (ka_t.reshape(b, H_A, n_tok, 2 * DA), va_t, jnp.stack([s_f, s_b], axis=1).astype(h.dtype))
        return out, new
    return out, None


def odd_mixer(h, l, p, ctx):
    j = l // 2
    b, n_tok, _ = h.shape
    is_context = ctx is None
    qkv, z, a, bb = split_cols(h @ p['w_in_odd'][j], ODD_SPLITS)
    qkv = jax.nn.silu(depthwise_conv_centred(qkv, p['conv_w'][j]))
    q, k, v = jnp.split(qkv, 3, axis=-1)
    q = l2_normalize(q.reshape(b, n_tok, H_C, DK_C)) * (DK_C ** -0.5)
    k = l2_normalize(k.reshape(b, n_tok, H_C, DK_C))
    v = v.reshape(b, n_tok, H_C, DV_C)
    a = a.reshape(b, n_tok, 2, H_C).astype(jnp.float32)
    bb = bb.reshape(b, n_tok, 2, H_C).astype(jnp.float32)
    g = -jnp.exp(p['dn_a_log'][j].astype(jnp.float32)) * jax.nn.softplus(a + p['dn_dt_bias'][j].astype(jnp.float32))
    beta = jax.nn.sigmoid(bb)
    if is_context:
        s0 = jnp.zeros((b, 2, H_C, DK_C, DV_C), jnp.float32)
    else:
        s0 = ctx
    o_f, s_f = gated_delta_chunked(q, k, v, g[:, :, 0], beta[:, :, 0], s0[:, 0])
    o_b, s_b = gated_delta_chunked(q[:, ::-1], k[:, ::-1], v[:, ::-1], g[:, ::-1, 1], beta[:, ::-1, 1], s0[:, 1])
    o = (o_f + o_b[:, ::-1]).astype(h.dtype)
    o = rms_norm(o, p['dn_norm'][j]) * jax.nn.silu(z.reshape(b, n_tok, H_C, DV_C))
    out = o.reshape(b, n_tok, D_C) @ p['w_out_odd'][j]
    if is_context:
        return out, jnp.stack([s_f, s_b], axis=1).astype(h.dtype)
    return out, None


def hier_moe(h, l, p):
    b, n_tok, d = h.shape
    x = h.reshape(b * n_tok, d)
    g_logits = (x @ p['moe_w_group'][l] + p['moe_b_group'][l]).astype(jnp.float32)
    g_w, g_idx = lax.top_k(jax.nn.softmax(g_logits, axis=-1), 1)
    e_logits = (x @ p['moe_w_router'][l] + p['moe_b_router'][l]).astype(jnp.float32)
    e_logits = e_logits.reshape(-1, N_GROUPS, E_PER_GROUP)
    e_logits = jnp.take_along_axis(e_logits, g_idx[:, :, None], axis=1)[:, 0]
    e_val, e_idx = lax.top_k(e_logits, TOP_K_INNER)
    w = g_w * jax.nn.softmax(e_val, axis=-1)
    expert_id = g_idx * E_PER_GROUP + e_idx
    gate = jnp.sum(jax.nn.one_hot(expert_id, N_EXPERTS, dtype=jnp.float32) * w[..., None], axis=1)
    hid = jax.nn.silu(jnp.einsum('td,edf->tef', x, p['moe_w1'][l])) * jnp.einsum('td,edf->tef', x, p['moe_w3'][l])
    y = jnp.einsum('tef,efd->td', hid * gate[..., None].astype(hid.dtype), p['moe_w2'][l])
    return y.reshape(b, n_tok, d)


def trunk_layer(l, x, cond, p, ctx):
    shift1, scale1, gate1, shift2, scale2, gate2 = adaln_params(cond, p['w_mod'][l], p['b_mod'][l])
    h = rms_norm(x, p['norm1'][l]) * (1.0 + scale1) + shift1
    if l % 2 == 0:
        out, new = even_mixer(h, l, p, ctx)
    else:
        out, new = odd_mixer(h, l, p, ctx)
    x = x + gate1 * out
    h = rms_norm(x, p['norm2'][l]) * (1.0 + scale2) + shift2
    x = x + gate2 * hier_moe(h, l, p)
    return x, new


def setup_inputs(seed: int = 0) -> dict:
    key = jax.random.key(seed)
    ks = jax.random.split(key, 32)
    f32 = jnp.float32

    def nrm(k, shape, s):
        return jax.random.normal(k, shape, f32) * s

    def gain(k, shape):
        return 1.0 + 0.01 * jax.random.normal(k, shape, f32)

    ret_base = jnp.log(2.0 ** (5.0 + jnp.arange(H_B, dtype=f32)) - 1.0)
    dt = jnp.exp(jax.random.uniform(ks[22], (N_ODD, 2, H_C), f32, math.log(1e-3), math.log(1e-1)))
    d_even_out = H_A * DV_A + H_B * DV_B
    return {
        'x_prompt': nrm(ks[0], (BATCH, SEQ, D_MODEL), 1.0),
        'x_sample': nrm(ks[1], (DEC_BATCH, DEC_SEQ, D_MODEL), 1.0),
        'cache_attn_k': nrm(ks[2], (DEC_BATCH, N_EVEN, H_A, PAST_LEN, 2 * DA), 1.0),
        'cache_attn_v': nrm(ks[3], (DEC_BATCH, N_EVEN, H_A, PAST_LEN, DV_A), 1.0),
        'state_ret': nrm(ks[4], (DEC_BATCH, N_EVEN, 2, H_B, DK_B, DV_B), 0.1),
        'state_delta': nrm(ks[5], (DEC_BATCH, N_ODD, 2, H_C, DK_C, DV_C), 0.1),
        'c': nrm(ks[6], (DEC_BATCH, D_MODEL), 1.0),
        'c_ctx': nrm(ks[7], (D_MODEL,), 1.0),
        'w_mod': nrm(ks[8], (DEPTH, D_MODEL, 6 * D_MODEL), 0.5 * D_MODEL ** -0.5),
        'b_mod': nrm(ks[9], (DEPTH, 6 * D_MODEL), 0.01),
        'norm1': gain(ks[10], (DEPTH, D_MODEL)),
        'norm2': gain(ks[11], (DEPTH, D_MODEL)),
        'w_in_even': nrm(ks[12], (N_EVEN, D_MODEL, sum(EVEN_SPLITS)), D_MODEL ** -0.5),
        'diff_lambda': nrm(ks[13], (N_EVEN, 4, DA), 0.1),
        'subln_gain': gain(ks[14], (N_EVEN, DV_A)),
        'ret_decay': ret_base + nrm(ks[15], (N_EVEN, 2, H_B), 0.1),
        'ret_norm': gain(ks[16], (N_EVEN, DV_B)),
        'w_out_even': nrm(ks[17], (N_EVEN, d_even_out, D_MODEL), d_even_out ** -0.5),
        'w_in_odd': nrm(ks[18], (N_ODD, D_MODEL, sum(ODD_SPLITS)), D_MODEL ** -0.5),
        'conv_w': nrm(ks[19], (N_ODD, CONV_K, 3 * D_C), CONV_K ** -0.5),
        'dn_a_log': jnp.log(jax.random.uniform(ks[20], (N_ODD, 2, H_C), f32, 1.0, 16.0)),
        'dn_dt_bias': dt + jnp.log(-jnp.expm1(-dt)),
        'dn_norm': gain(ks[21], (N_ODD, DV_C)),
        'w_out_odd': nrm(ks[23], (N_ODD, D_C, D_MODEL), D_C ** -0.5),
        'moe_w_group': nrm(ks[24], (DEPTH, D_MODEL, N_GROUPS), D_MODEL ** -0.5),
        'moe_b_group': nrm(ks[25], (DEPTH, N_GROUPS), 0.01),
        'moe_w_router': nrm(ks[26], (DEPTH, D_MODEL, N_EXPERTS), D_MODEL ** -0.5),
        'moe_b_router': nrm(ks[27], (DEPTH, N_EXPERTS), 0.01),
        'moe_w1': nrm(ks[28], (DEPTH, N_EXPERTS, D_MODEL, EXPERT_FF), D_MODEL ** -0.5),
        'moe_w3': nrm(ks[29], (DEPTH, N_EXPERTS, D_MODEL, EXPERT_FF), D_MODEL ** -0.5),
        'moe_w2': nrm(ks[30], (DEPTH, N_EXPERTS, EXPERT_FF, D_MODEL), EXPERT_FF ** -0.5),
        'final_norm': gain(ks[31], (D_MODEL,)),
    }


def reference(x_prompt, x_sample, cache_attn_k, cache_attn_v, state_ret, state_delta, c, c_ctx,
              w_mod, b_mod, norm1, norm2, w_in_even, diff_lambda, subln_gain, ret_decay, ret_norm,
              w_out_even, w_in_odd, conv_w, dn_a_log, dn_dt_bias, dn_norm, w_out_odd,
              moe_w_group, moe_b_group, moe_w_router, moe_b_router, moe_w1, moe_w3, moe_w2, final_norm):
    p = dict(w_mod=w_mod, b_mod=b_mod, norm1=norm1, norm2=norm2, w_in_even=w_in_even,
             diff_lambda=diff_lambda, subln_gain=subln_gain, ret_decay=ret_decay, ret_norm=ret_norm,
             w_out_even=w_out_even, w_in_odd=w_in_odd, conv_w=conv_w, dn_a_log=dn_a_log,
             dn_dt_bias=dn_dt_bias, dn_norm=dn_norm, w_out_odd=w_out_odd, moe_w_group=moe_w_group,
             moe_b_group=moe_b_group, moe_w_router=moe_w_router, moe_b_router=moe_b_router,
             moe_w1=moe_w1, moe_w3=moe_w3, moe_w2=moe_w2)

    x = x_prompt
    cond_ctx = c_ctx[None, :]
    ks_list, vs_list, ret_list, delta_list = [], [], [], []
    for l in range(DEPTH):
        x, new = trunk_layer(l, x, cond_ctx, p, None)
        if l % 2 == 0:
            ks_list.append(new[0])
            vs_list.append(new[1])
            ret_list.append(new[2])
        else:
            delta_list.append(new)
    y_prompt = rms_norm(x, final_norm)
    new_attn_k = jnp.stack(ks_list, axis=1)
    new_attn_v = jnp.stack(vs_list, axis=1)
    new_state_ret = jnp.stack(ret_list, axis=1)
    new_state_delta = jnp.stack(delta_list, axis=1)

    x = x_sample
    for l in range(DEPTH):
        i = l // 2
        if l % 2 == 0:
            ctx = (cache_attn_k[:, i], cache_attn_v[:, i], state_ret[:, i])
        else:
            ctx = state_delta[:, i]
        x, _ = trunk_layer(l, x, c, p, ctx)
    y_sample = rms_norm(x, final_norm)

    return (y_prompt, y_sample, new_attn_k, new_attn_v, new_state_ret, new_state_delta)
```

```python
import functools
import math

import jax
import jax.numpy as jnp
from jax import lax
from jax.experimental import pallas as pl
from jax.experimental.pallas import tpu as pltpu

F32 = jnp.float32
BF16 = jnp.bfloat16
HI = lax.Precision.HIGHEST

GRID_W = 64
DA = 64
HEAD = 128
CONV_K = 5
CHUNK = 64
ROPE_BASE = 10000.0
N_GROUPS = 4
E_PER_GROUP = 4
N_EXPERTS = N_GROUPS * E_PER_GROUP
EPS = 1e-6

LANES = 128
SUBLANES = 8
MAX_CONDS = SUBLANES
ROW_TILE = 256
MOE_TILE = 256
VMEM_LIMIT = 56 * 1024 * 1024


def _cparams(sem):
    return pltpu.CompilerParams(dimension_semantics=sem, vmem_limit_bytes=VMEM_LIMIT)


def _bdot(a, b):
    return jnp.dot(a.astype(BF16), b.astype(BF16), preferred_element_type=F32)


def _bdot_nt(a, b):
    return lax.dot_general(a.astype(BF16), b.astype(BF16), (((1,), (1,)), ((), ())),
                           preferred_element_type=F32)


def _bdot_tn(a, b):
    return lax.dot_general(a.astype(BF16), b.astype(BF16), (((0,), (0,)), ((), ())),
                           preferred_element_type=F32)


def _hdot(a, b):
    return jnp.dot(a, b, precision=HI, preferred_element_type=F32)


def _sigmoid(x):
    return 1.0 / (1.0 + jnp.exp(-x))


def _silu(x):
    return x * _sigmoid(x)


def _softplus(x):
    return jnp.maximum(x, 0.0) + jnp.log(1.0 + jnp.exp(-jnp.abs(x)))


def _rms(x, gain):
    return x * lax.rsqrt(jnp.mean(x * x, axis=-1, keepdims=True) + EPS) * gain


def _softmax(s):
    m = jnp.max(s, axis=-1, keepdims=True)
    e = jnp.exp(s - m)
    return e / jnp.sum(e, axis=-1, keepdims=True)


def _mod_kernel(c_ref, w_ref, b_ref, o_ref):
    c = c_ref[...]
    o_ref[0] = _bdot(_silu(c), w_ref[0]) + b_ref[0]


def _modulation(cond, w_mod, b_mod, tn=1024):
    depth, d, n = w_mod.shape
    return pl.pallas_call(
        _mod_kernel,
        grid=(depth, n // tn),
        in_specs=[pl.BlockSpec((MAX_CONDS, d), lambda l, j: (0, 0)),
                  pl.BlockSpec((1, d, tn), lambda l, j: (l, 0, j)),
                  pl.BlockSpec((1, 1, tn), lambda l, j: (l, 0, j))],
        out_specs=pl.BlockSpec((1, MAX_CONDS, tn), lambda l, j: (l, 0, j)),
        out_shape=jax.ShapeDtypeStruct((depth, MAX_CONDS, n), F32),
        compiler_params=_cparams(("parallel", "parallel")),
        name="adaln_mod",
    )(cond, w_mod, b_mod.reshape(depth, 1, n))


class _Tokens:
    def __init__(self, n_ctx, l_ctx, n_lat, l_lat):
        self.n_ctx, self.l_ctx, self.n_lat, self.l_lat = n_ctx, l_ctx, n_lat, l_lat
        self.t_ctx = n_ctx * l_ctx
        self.t = self.t_ctx + n_lat * l_lat

    def cond_row(self, i, tm):
        n0 = self.t_ctx // tm
        return jnp.where(i < n0, 0, 1 + (i - n0) // (self.l_lat // tm))


def _mod_spec(tok, tm, layer, which, d):
    base = (layer * 6 + which) * MAX_CONDS
    return pl.BlockSpec((1, 1, d), lambda i, *_: (base + tok.cond_row(i, tm), 0, 0))


def _inproj_kernel(x_ref, g_ref, sh_ref, sc_ref, w_ref, o_ref, h_scr):
    @pl.when(pl.program_id(1) == 0)
    def _():
        h = _rms(x_ref[...], g_ref[...]) * (1.0 + sc_ref[0]) + sh_ref[0]
        h_scr[...] = h.astype(BF16)

    o_ref[...] = jnp.dot(h_scr[...], w_ref[...], preferred_element_type=F32)


def _in_proj(tok, x, gain, modr, layer, w, tm, tn):
    t, d = x.shape
    n = w.shape[1]
    return pl.pallas_call(
        _inproj_kernel,
        grid=(t // tm, n // tn),
        in_specs=[pl.BlockSpec((tm, d), lambda i, j: (i, 0)),
                  pl.BlockSpec((1, d), lambda i, j: (0, 0)),
                  _mod_spec(tok, tm, layer, 0, d),
                  _mod_spec(tok, tm, layer, 1, d),
                  pl.BlockSpec((d, tn), lambda i, j: (0, j))],
        out_specs=pl.BlockSpec((tm, tn), lambda i, j: (i, j)),
        out_shape=jax.ShapeDtypeStruct((t, n), F32),
        scratch_shapes=[pltpu.VMEM((tm, d), BF16)],
        compiler_params=_cparams(("parallel", "arbitrary")),
        name="in_proj",
    )(x, gain.reshape(1, d), modr, modr, w)


def _outproj_kernel(n_in, *refs):
    o_refs = refs[:n_in]
    w_refs = refs[n_in:2 * n_in]
    x_ref, gate_ref, g2_ref, sh_ref, sc_ref, xo_ref, h_ref = refs[2 * n_in:]
    acc = _bdot(o_refs[0][...], w_refs[0][...])
    for o_r, w_r in zip(o_refs[1:], w_refs[1:]):
        acc = acc + _bdot(o_r[...], w_r[...])
    x = x_ref[...] + gate_ref[0] * acc
    xo_ref[...] = x
    h = _rms(x, g2_ref[...]) * (1.0 + sc_ref[0]) + sh_ref[0]
    h_ref[...] = h


def _out_proj(tok, outs, ws, x, gain2, modr, layer, tm=ROW_TILE):
    t, d = x.shape
    n_in = len(outs)
    in_specs = ([pl.BlockSpec((tm, o.shape[1]), lambda i: (i, 0)) for o in outs]
                + [pl.BlockSpec(w.shape, lambda i: (0, 0)) for w in ws]
                + [pl.BlockSpec((tm, d), lambda i: (i, 0)),
                   _mod_spec(tok, tm, layer, 2, d),
                   pl.BlockSpec((1, d), lambda i: (0, 0)),
                   _mod_spec(tok, tm, layer, 3, d),
                   _mod_spec(tok, tm, layer, 4, d)])
    return pl.pallas_call(
        functools.partial(_outproj_kernel, n_in),
        grid=(t // tm,),
        in_specs=in_specs,
        out_specs=[pl.BlockSpec((tm, d), lambda i: (i, 0)),
                   pl.BlockSpec((tm, d), lambda i: (i, 0))],
        out_shape=[jax.ShapeDtypeStruct((t, d), F32), jax.ShapeDtypeStruct((t, d), F32)],
        compiler_params=_cparams(("parallel",)),
        name="out_proj",
    )(*outs, *ws, x, modr, gain2.reshape(1, d), modr, modr)


def _lambda(dl_ref, lam_init):
    dl = dl_ref[0]
    s1 = jnp.sum(dl[0:1] * dl[1:2], axis=-1, keepdims=True)
    s2 = jnp.sum(dl[2:3] * dl[3:4], axis=-1, keepdims=True)
    return jnp.exp(s1) - jnp.exp(s2) + lam_init


def _log_gammas(rdec_ref):
    lg = -_softplus(-rdec_ref[0])
    return lg[0:1, 0:1], lg[1:2, 0:1]


def _diff_attn(q, k1, k2, v, lam):
    p1 = _softmax(_bdot_nt(q[:, :DA], k1) * (DA ** -0.5))
    p2 = _softmax(_bdot_nt(q[:, DA:], k2) * (DA ** -0.5))
    return _bdot(p1 - lam * p2, v)


def _ret_mask(row0, nq, nk, lg_f, lg_b):
    qi = lax.broadcasted_iota(jnp.int32, (nq, nk), 0) + row0
    kj = lax.broadcasted_iota(jnp.int32, (nq, nk), 1)
    diff = (qi - kj).astype(F32)
    fwd = jnp.where(diff >= 0, jnp.exp(jnp.maximum(diff, 0.0) * lg_f), 0.0)
    bwd = jnp.where(diff <= 0, jnp.exp(jnp.maximum(-diff, 0.0) * lg_b), 0.0)
    return fwd + bwd


def _even_ctx_kernel(lam_init, qa_ref, ka_ref, va_ref, qb_ref, kb_ref, vb_ref, gb_ref,
                     dl_ref, subln_ref, retn_ref, rdec_ref,
                     oa_ref, or_ref, ko_ref, vo_ref, so_ref):
    n = qa_ref.shape[0]
    lam = _lambda(dl_ref, lam_init)
    ka = ka_ref[...]
    va = va_ref[...]
    ko_ref[0, 0, 0] = ka
    vo_ref[0, 0, 0] = va
    kab = ka.astype(BF16)
    o = _diff_attn(qa_ref[...], kab[:, :DA], kab[:, DA:], va.astype(BF16), lam)
    oa_ref[...] = _rms(o, subln_ref[...]) * (1.0 - lam_init)

    lg_f, lg_b = _log_gammas(rdec_ref)
    kb = kb_ref[...] * (HEAD ** -0.5)
    vb = vb_ref[...].astype(BF16)
    a = _bdot_nt(qb_ref[...], kb) * _ret_mask(0, n, n, lg_f, lg_b)
    o = _bdot(a, vb)
    or_ref[...] = _rms(o, retn_ref[...]) * _silu(gb_ref[...])
    pos = lax.broadcasted_iota(jnp.int32, (n, 1), 0).astype(F32)
    so_ref[0, 0, 0, 0] = _bdot_tn(kb * jnp.exp((n - 1.0 - pos) * lg_f), vb)
    so_ref[0, 0, 1, 0] = _bdot_tn(kb * jnp.exp(pos * lg_b), vb)


def _head_spec(rows, col0, row0=0):
    return pl.BlockSpec((rows, HEAD), lambda s, h: (row0 + s, col0 + h))


def _even_param_specs():
    return [pl.BlockSpec((1, 4, DA), lambda s, h: (0, 0, 0)),
            pl.BlockSpec((1, HEAD), lambda s, h: (0, 0)),
            pl.BlockSpec((1, HEAD), lambda s, h: (0, 0)),
            pl.BlockSpec((1, 2, LANES), lambda s, h: (h, 0, 0))]


def _even_params(dl, subln, retn, rdec):
    n_heads = rdec.shape[1]
    rdec_b = jnp.broadcast_to(rdec.T[:, :, None], (n_heads, 2, LANES))
    return dl.reshape(1, 4, DA), subln.reshape(1, HEAD), retn.reshape(1, HEAD), rdec_b


def _even_ctx(tok, p, layer, dl, subln, retn, rdec):
    n_heads = p.shape[1] // (7 * HEAD)
    lam_init = 0.8 - 0.6 * math.exp(-0.3 * layer)
    l, nseq = tok.l_ctx, tok.n_ctx
    dh = n_heads * HEAD
    return pl.pallas_call(
        functools.partial(_even_ctx_kernel, lam_init),
        grid=(nseq, n_heads),
        in_specs=[_head_spec(l, c * n_heads) for c in range(7)] + _even_param_specs(),
        out_specs=[pl.BlockSpec((l, HEAD), lambda s, h: (s, h)),
                   pl.BlockSpec((l, HEAD), lambda s, h: (s, h)),
                   pl.BlockSpec((1, 1, 1, l, HEAD), lambda s, h: (s, 0, h, 0, 0)),
                   pl.BlockSpec((1, 1, 1, l, HEAD), lambda s, h: (s, 0, h, 0, 0)),
                   pl.BlockSpec((1, 1, 2, 1, HEAD, HEAD), lambda s, h: (s, 0, 0, h, 0, 0))],
        out_shape=[jax.ShapeDtypeStruct((tok.t_ctx, dh), F32),
                   jax.ShapeDtypeStruct((tok.t_ctx, dh), F32),
                   jax.ShapeDtypeStruct((nseq, 1, n_heads, l, HEAD), F32),
                   jax.ShapeDtypeStruct((nseq, 1, n_heads, l, HEAD), F32),
                   jax.ShapeDtypeStruct((nseq, 1, 2, n_heads, HEAD, HEAD), F32)],
        compiler_params=_cparams(("parallel", "parallel")),
        name="even_mixer_ctx",
    )(*([p] * 7), *_even_params(dl, subln, retn, rdec))


def _rope(x, cos, sin_signed):
    half = DA // 2
    lane = lax.broadcasted_iota(jnp.int32, x.shape, 1)
    swapped = jnp.where((lane % DA) < half,
                        pltpu.roll(x, x.shape[1] - half, axis=1),
                        pltpu.roll(x, half, axis=1))
    return x * cos + swapped * sin_signed


def _even_lat_kernel(lam_init, tq, qa_ref, ka_ref, va_ref, qb_ref, kb_ref, vb_ref, gb_ref,
                     ck_ref, cv_ref, s0_ref, cos_ref, sin_ref,
                     dl_ref, subln_ref, retn_ref, rdec_ref,
                     oa_ref, or_ref, kall, vall):
    n = qa_ref.shape[0]
    past = ck_ref.shape[3]
    lam = _lambda(dl_ref, lam_init)
    cos = cos_ref[...]
    sin = sin_ref[...]
    kall[0:past, :] = ck_ref[0, 0, 0].astype(BF16)
    kall[past:past + n, :] = _rope(ka_ref[...], cos, sin).astype(BF16)
    vall[0:past, :] = cv_ref[0, 0, 0].astype(BF16)
    vall[past:past + n, :] = va_ref[...].astype(BF16)

    lg_f, lg_b = _log_gammas(rdec_ref)
    kb = (kb_ref[...] * (HEAD ** -0.5)).astype(BF16)
    vb = vb_ref[...].astype(BF16)
    s0f = s0_ref[0, 0, 0, 0].astype(BF16)
    s0b = s0_ref[0, 0, 1, 0].astype(BF16)
    for blk in range(n // tq):
        rows = pl.ds(blk * tq, tq)
        q = _rope(qa_ref[rows, :], cos[blk * tq:(blk + 1) * tq], sin[blk * tq:(blk + 1) * tq])
        o = _diff_attn(q, kall[:, :DA], kall[:, DA:], vall[...], lam)
        oa_ref[rows, :] = _rms(o, subln_ref[...]) * (1.0 - lam_init)

        qb = qb_ref[rows, :]
        a = _bdot_nt(qb, kb) * _ret_mask(blk * tq, tq, n, lg_f, lg_b)
        pos = (lax.broadcasted_iota(jnp.int32, (tq, 1), 0) + blk * tq).astype(F32)
        o = (_bdot(a, vb) + _bdot(qb * jnp.exp((pos + 1.0) * lg_f), s0f)
             + _bdot(qb * jnp.exp((n - pos) * lg_b), s0b))
        or_ref[rows, :] = _rms(o, retn_ref[...]) * _silu(gb_ref[rows, :])


def _rope_tables(n_tok):
    n_freq = DA // 4
    t = jnp.arange(n_tok)
    inv_freq = ROPE_BASE ** (-jnp.arange(n_freq, dtype=F32) / n_freq)
    ang = jnp.concatenate([(t // GRID_W).astype(F32)[:, None] * inv_freq,
                           (t % GRID_W).astype(F32)[:, None] * inv_freq], axis=-1)
    cos, sin = jnp.cos(ang), jnp.sin(ang)
    return jnp.tile(cos, (1, 4)), jnp.tile(jnp.concatenate([-sin, sin], axis=-1), (1, 2))


def _even_lat(tok, p, layer, i, dl, subln, retn, rdec, cache_k, cache_v, state_ret, tq=256):
    n_heads = p.shape[1] // (7 * HEAD)
    lam_init = 0.8 - 0.6 * math.exp(-0.3 * layer)
    l, nseq = tok.l_lat, tok.n_lat
    row0 = tok.t_ctx // l
    past = cache_k.shape[3]
    dh = n_heads * HEAD
    cos, sin = _rope_tables(l)
    return pl.pallas_call(
        functools.partial(_even_lat_kernel, lam_init, tq),
        grid=(nseq, n_heads),
        in_specs=[_head_spec(l, c * n_heads, row0) for c in range(7)]
        + [pl.BlockSpec((1, 1, 1, past, HEAD), lambda s, h: (s, i, h, 0, 0)),
           pl.BlockSpec((1, 1, 1, past, HEAD), lambda s, h: (s, i, h, 0, 0)),
           pl.BlockSpec((1, 1, 2, 1, HEAD, HEAD), lambda s, h: (s, i, 0, h, 0, 0)),
           pl.BlockSpec((l, HEAD), lambda s, h: (0, 0)),
           pl.BlockSpec((l, HEAD), lambda s, h: (0, 0))]
        + _even_param_specs(),
        out_specs=[pl.BlockSpec((l, HEAD), lambda s, h: (s, h)),
                   pl.BlockSpec((l, HEAD), lambda s, h: (s, h))],
        out_shape=[jax.ShapeDtypeStruct((nseq * l, dh), F32),
                   jax.ShapeDtypeStruct((nseq * l, dh), F32)],
        scratch_shapes=[pltpu.VMEM((past + l, HEAD), BF16), pltpu.VMEM((past + l, HEAD), BF16)],
        compiler_params=_cparams(("parallel", "parallel")),
        name="even_mixer_lat",
    )(*([p] * 7), cache_k, cache_v, state_ret, cos, sin, *_even_params(dl, subln, retn, rdec))


def _conv_silu(x, w):
    n = x.shape[0]
    pad = (CONV_K - 1) // 2
    row = lax.broadcasted_iota(jnp.int32, x.shape, 0)
    acc = x * w[pad:pad + 1]
    for j in range(CONV_K):
        d = j - pad
        if d == 0:
            continue
        shifted = pltpu.roll(x, (-d) % n, axis=0)
        valid = (row + d >= 0) & (row + d < n)
        acc = acc + jnp.where(valid, shifted, 0.0) * w[j:j + 1]
    return _silu(acc)


def _l2n(x):
    return x * lax.rsqrt(jnp.sum(x * x, axis=-1, keepdims=True) + EPS)


def _unit_tri_inverse(lm):
    c = lm.shape[0]
    eye = (lax.broadcasted_iota(jnp.int32, (c, c), 0)
           == lax.broadcasted_iota(jnp.int32, (c, c), 1)).astype(F32)
    pw = -lm
    inv = eye + pw
    span = 1
    while 2 * span < c:
        pw = _hdot(pw, pw)
        inv = inv + _hdot(inv, pw)
        span *= 2
    return inv


def _delta_kernel(has_s0, q_ref, k_ref, v_ref, z_ref, wq_ref, wk_ref, wv_ref,
                  gc_ref, gr_ref, par_ref, gain_ref, *rest):
    if has_s0:
        s0_ref, o_ref, qs, ks, vs, gcol, grow, oacc = rest
        so_ref = None
    else:
        o_ref, so_ref, qs, ks, vs, gcol, grow, oacc = rest
        s0_ref = None
    n = q_ref.shape[0]
    n_chunks = n // CHUNK
    c = CHUNK

    qs[...] = _l2n(_conv_silu(q_ref[...], wq_ref[...])) * (HEAD ** -0.5)
    ks[...] = _l2n(_conv_silu(k_ref[...], wk_ref[...]))
    vs[...] = _conv_silu(v_ref[...], wv_ref[...])

    par = par_ref[0]
    neg_a_f, neg_a_b = -jnp.exp(par[0:1, 0:1]), -jnp.exp(par[1:2, 0:1])
    dt_f, dt_b = par[2:3, 0:1], par[3:4, 0:1]

    def gates(x, idx):
        g = jnp.where(idx == 0, neg_a_f, neg_a_b) * _softplus(x + jnp.where(idx == 0, dt_f, dt_b))
        return jnp.where(idx < 2, g, _sigmoid(x))

    xc = gc_ref[0]
    gcol[...] = gates(xc, lax.broadcasted_iota(jnp.int32, xc.shape, 1))
    xr = gr_ref[0]
    grow[...] = gates(xr, lax.broadcasted_iota(jnp.int32, xr.shape, 1))

    ri = lax.broadcasted_iota(jnp.int32, (c, c), 0)
    ci = lax.broadcasted_iota(jnp.int32, (c, c), 1)
    gain = gain_ref[...]

    for d in range(2):
        incl = (ri >= ci) if d == 0 else (ri <= ci)
        strict = (ri > ci) if d == 0 else (ri < ci)
        incl_t = (ci >= ri) if d == 0 else (ci <= ri)

        def body(step, s, d=d, incl=incl, strict=strict, incl_t=incl_t):
            cidx = step if d == 0 else n_chunks - 1 - step
            rows = pl.ds(pl.multiple_of(cidx * c, c), c)
            qc, kc, vc = qs[rows, :], ks[rows, :], vs[rows, :]
            gcl = gcol[rows, :]
            g_c, b_c = gcl[:, d:d + 1], gcl[:, 2 + d:3 + d]
            g_r = grow[cidx][d:d + 1, :]
            gcc = jnp.sum(jnp.where(incl, g_r, 0.0), axis=1, keepdims=True)
            gcr = jnp.sum(jnp.where(incl_t, g_c, 0.0), axis=0, keepdims=True)
            decay = jnp.where(incl, jnp.exp(jnp.where(incl, gcc - gcr, 0.0)), 0.0)
            kk = _bdot_nt(kc, kc)
            inv = _unit_tri_inverse(jnp.where(strict, b_c * kk * decay, 0.0))
            rhs = jnp.concatenate([vc * b_c, kc * (b_c * jnp.exp(gcc))], axis=-1)
            sol = _hdot(inv, rhs)
            u, w = sol[:, :HEAD], sol[:, HEAD:]
            v_new = u - _bdot(w, s)
            qk = jnp.where(incl, _bdot_nt(qc, kc) * decay, 0.0)
            o = _bdot(qc * jnp.exp(gcc), s) + _bdot(qk, v_new)
            if d == 0:
                oacc[rows, :] = o
            else:
                oacc[rows, :] = oacc[rows, :] + o
            g_last = jnp.sum(g_c, axis=0, keepdims=True)
            return s * jnp.exp(g_last) + _bdot_tn(kc * jnp.exp(g_last - gcc), v_new)

        s_init = s0_ref[0, 0, d, 0] if has_s0 else jnp.zeros((HEAD, HEAD), F32)
        s_fin = lax.fori_loop(0, n_chunks, body, s_init)
        if so_ref is not None:
            so_ref[0, 0, d, 0] = s_fin

    o_ref[...] = _rms(oacc[...], gain) * _silu(z_ref[...])


def _delta(tok, p, gc, gr, par, conv_w, gain, ctx, state=None, j=0):
    n_heads = par.shape[0]
    dc = n_heads * HEAD
    l, nseq = (tok.l_ctx, tok.n_ctx) if ctx else (tok.l_lat, tok.n_lat)
    row0 = 0 if ctx else tok.t_ctx // l
    in_specs = ([_head_spec(l, c * n_heads, row0) for c in range(4)]
                + [pl.BlockSpec((CONV_K, HEAD), lambda s, h, c=c: (0, c * n_heads + h)) for c in range(3)]
                + [pl.BlockSpec((1, l, SUBLANES), lambda s, h: (h, row0 + s, 0)),
                   pl.BlockSpec((1, l // CHUNK, SUBLANES, CHUNK), lambda s, h: (h, row0 + s, 0, 0)),
                   pl.BlockSpec((1, 4, LANES), lambda s, h: (h, 0, 0)),
                   pl.BlockSpec((1, HEAD), lambda s, h: (0, 0))])
    args = [p, p, p, p, conv_w, conv_w, conv_w, gc, gr, par, gain.reshape(1, HEAD)]
    out_specs = [pl.BlockSpec((l, HEAD), lambda s, h: (s, h))]
    out_shape = [jax.ShapeDtypeStruct((nseq * l, dc), F32)]
    if ctx:
        out_specs.append(pl.BlockSpec((1, 1, 2, 1, HEAD, HEAD), lambda s, h: (s, 0, 0, h, 0, 0)))
        out_shape.append(jax.ShapeDtypeStruct((nseq, 1, 2, n_heads, HEAD, HEAD), F32))
    else:
        in_specs.append(pl.BlockSpec((1, 1, 2, 1, HEAD, HEAD), lambda s, h: (s, j, 0, h, 0, 0)))
        args.append(state)
    return pl.pallas_call(
        functools.partial(_delta_kernel, not ctx),
        grid=(nseq, n_heads),
        in_specs=in_specs,
        out_specs=out_specs,
        out_shape=out_shape,
        scratch_shapes=[pltpu.VMEM((l, HEAD), F32), pltpu.VMEM((l, HEAD), F32), pltpu.VMEM((l, HEAD), F32),
                        pltpu.VMEM((l, SUBLANES), F32), pltpu.VMEM((l // CHUNK, SUBLANES, CHUNK), F32),
                        pltpu.VMEM((l, HEAD), F32)],
        compiler_params=_cparams(("parallel", "parallel")),
        name="delta_ctx" if ctx else "delta_lat",
    )(*args)


def _delta_gate_layouts(ab, n_heads):
    t = ab.shape[0]
    x = ab[:, :4 * n_heads].reshape(t, 4, n_heads)
    x = jnp.pad(x, ((0, 0), (0, SUBLANES - 4), (0, 0)))
    gc = x.transpose(2, 0, 1)
    gr = x.reshape(t // CHUNK, CHUNK, SUBLANES, n_heads).transpose(3, 0, 2, 1)
    return gc, gr


def _router_kernel(h_ref, w_ref, b_ref, eid_ref, wt_ref):
    h = h_ref[...]
    w = w_ref[...]
    h_hi = h.astype(BF16)
    h_lo = (h - h_hi.astype(F32)).astype(BF16)
    w_hi = w.astype(BF16)
    w_lo = (w - w_hi.astype(F32)).astype(BF16)
    lt = _bdot_nt(w_hi, h_hi) + (_bdot_nt(w_lo, h_hi) + _bdot_nt(w_hi, h_lo)) + b_ref[...]
    tm = lt.shape[1]

    def first_argmax(x, valid=None):
        rows = lax.broadcasted_iota(jnp.int32, x.shape, 0)
        if valid is not None:
            x = jnp.where(valid, x, -jnp.inf)
        m = jnp.max(x, axis=0, keepdims=True)
        idx = jnp.min(jnp.where(x == m, rows, x.shape[0]), axis=0, keepdims=True)
        return m, idx

    gl = lt[0:N_GROUPS]
    gmax, gidx = first_argmax(gl)
    g_w = 1.0 / jnp.sum(jnp.exp(gl - gmax), axis=0, keepdims=True)
    el = jnp.zeros((E_PER_GROUP, tm), F32)
    for g in range(N_GROUPS):
        lo = N_GROUPS + g * E_PER_GROUP
        el = jnp.where(gidx == g, lt[lo:lo + E_PER_GROUP], el)
    v1, i1 = first_argmax(el)
    rows = lax.broadcasted_iota(jnp.int32, el.shape, 0)
    v2, i2 = first_argmax(el, rows != i1)
    e21 = jnp.exp(v2 - v1)
    p1 = 1.0 / (1.0 + e21)
    eid_ref[...] = jnp.concatenate([gidx * E_PER_GROUP + i1, gidx * E_PER_GROUP + i2], axis=0)[None]
    wt_ref[...] = jnp.concatenate([g_w * p1, g_w * (e21 * p1)], axis=0)[None]


def _router(h, w_rt, b_rt, tm=ROW_TILE):
    t, d = h.shape
    nt = t // tm
    nr = w_rt.shape[0]
    eid, wt = pl.pallas_call(
        _router_kernel,
        grid=(nt,),
        in_specs=[pl.BlockSpec((tm, d), lambda i: (i, 0)),
                  pl.BlockSpec((nr, d), lambda i: (0, 0)),
                  pl.BlockSpec((nr, 1), lambda i: (0, 0))],
        out_specs=[pl.BlockSpec((1, 2, tm), lambda i: (i, 0, 0)),
                   pl.BlockSpec((1, 2, tm), lambda i: (i, 0, 0))],
        out_shape=[jax.ShapeDtypeStruct((nt, 2, tm), jnp.int32),
                   jax.ShapeDtypeStruct((nt, 2, tm), F32)],
        compiler_params=_cparams(("parallel",)),
        name="moe_router",
    )(h, w_rt, b_rt)
    return (eid.transpose(0, 2, 1).reshape(t, 2), wt.transpose(0, 2, 1).reshape(t, 2))


def _moe_kernel(tile_e_ref, n_used_ref, src_ref, h_hbm, rw_ref, w1_ref, w3_ref, w2_ref, y_ref, xbuf, sem):
    i = pl.program_id(0)
    tm = xbuf.shape[0]

    def row_copy(r):
        return pltpu.make_async_copy(h_hbm.at[pl.ds(src_ref[0, 0, r], 1), :], xbuf.at[pl.ds(r, 1), :], sem)

    @pl.when(i < n_used_ref[0])
    def _():
        def start(r, carry):
            row_copy(r).start()
            return carry

        def wait(r, carry):
            row_copy(r).wait()
            return carry

        lax.fori_loop(0, tm, start, 0)
        lax.fori_loop(0, tm, wait, 0)
        x = xbuf[...].astype(BF16)
        a = jnp.dot(x, w1_ref[0], preferred_element_type=F32)
        b = jnp.dot(x, w3_ref[0], preferred_element_type=F32)
        hid = _silu(a) * b * rw_ref[...]
        y_ref[...] = jnp.dot(hid.astype(BF16), w2_ref[0], preferred_element_type=F32)

    @pl.when(i >= n_used_ref[0])
    def _():
        y_ref[...] = jnp.zeros_like(y_ref)


def _moe_experts(h, tile_e, n_used, src, row_w, w1, w3, w2, tm=MOE_TILE):
    t, d = h.shape
    n_tiles = src.shape[0]
    ff = w1.shape[2]
    grid_spec = pltpu.PrefetchScalarGridSpec(
        num_scalar_prefetch=2,
        grid=(n_tiles,),
        in_specs=[pl.BlockSpec((1, 1, tm), lambda i, te, nu: (i, 0, 0), memory_space=pltpu.SMEM),
                  pl.BlockSpec(memory_space=pl.ANY),
                  pl.BlockSpec((tm, 1), lambda i, te, nu: (i, 0)),
                  pl.BlockSpec((1, d, ff), lambda i, te, nu: (te[i], 0, 0)),
                  pl.BlockSpec((1, d, ff), lambda i, te, nu: (te[i], 0, 0)),
                  pl.BlockSpec((1, ff, d), lambda i, te, nu: (te[i], 0, 0))],
        out_specs=pl.BlockSpec((tm, d), lambda i, te, nu: (i, 0)),
        scratch_shapes=[pltpu.VMEM((tm, d), F32), pltpu.SemaphoreType.DMA(())],
    )
    return pl.pallas_call(
        _moe_kernel,
        grid_spec=grid_spec,
        out_shape=jax.ShapeDtypeStruct((n_tiles * tm, d), F32),
        compiler_params=_cparams(("arbitrary",)),
        name="moe_experts",
    )(tile_e, n_used, src, h, row_w, w1, w3, w2)


def _combine_kernel(final, pos_ref, ys_hbm, x_ref, gate_ref, fg_ref, xo_ref, *rest):
    if final:
        yo_ref, buf, sem = rest
    else:
        buf, sem = rest
    tm = x_ref.shape[0]

    def row_copy(k, r):
        return pltpu.make_async_copy(ys_hbm.at[pl.ds(pos_ref[0, k, r], 1), :], buf.at[k, pl.ds(r, 1), :], sem)

    def start(r, carry):
        row_copy(0, r).start()
        row_copy(1, r).start()
        return carry

    def wait(r, carry):
        row_copy(0, r).wait()
        row_copy(1, r).wait()
        return carry

    lax.fori_loop(0, tm, start, 0)
    lax.fori_loop(0, tm, wait, 0)
    x = x_ref[...] + gate_ref[0] * (buf[0] + buf[1])
    xo_ref[...] = x
    if final:
        yo_ref[...] = _rms(x, fg_ref[...])


def _moe_combine(tok, ys, pos, x, modr, layer, final_gain, final, tm=ROW_TILE):
    t, d = x.shape
    out_specs = [pl.BlockSpec((tm, d), lambda i: (i, 0))]
    out_shape = [jax.ShapeDtypeStruct((t, d), F32)]
    if final:
        out_specs.append(pl.BlockSpec((tm, d), lambda i: (i, 0)))
        out_shape.append(jax.ShapeDtypeStruct((t, d), F32))
    return pl.pallas_call(
        functools.partial(_combine_kernel, final),
        grid=(t // tm,),
        in_specs=[pl.BlockSpec((1, 2, tm), lambda i: (i, 0, 0), memory_space=pltpu.SMEM),
                  pl.BlockSpec(memory_space=pl.ANY),
                  pl.BlockSpec((tm, d), lambda i: (i, 0)),
                  _mod_spec(tok, tm, layer, 5, d),
                  pl.BlockSpec((1, d), lambda i: (0, 0))],
        out_specs=out_specs,
        out_shape=out_shape,
        scratch_shapes=[pltpu.VMEM((2, tm, d), F32), pltpu.SemaphoreType.DMA(())],
        compiler_params=_cparams(("arbitrary",)),
        name="moe_combine",
    )(pos, ys, x, modr, final_gain.reshape(1, d))


def _routing_tables(eid, wt, tm):
    t = eid.shape[0]
    n_tiles = (2 * t) // tm + N_EXPERTS
    flat_e = eid.reshape(-1)
    onehot = (flat_e[:, None] == jnp.arange(N_EXPERTS)[None, :]).astype(jnp.int32)
    csum = jnp.cumsum(onehot, axis=0)
    rank = jnp.sum(onehot * (csum - 1), axis=1)
    counts = csum[-1]
    tiles_per_e = (counts + tm - 1) // tm
    tile_end = jnp.cumsum(tiles_per_e)
    slot0 = (tile_end - tiles_per_e) * tm
    pos = slot0[flat_e] + rank
    n_used = tile_end[-1:].astype(jnp.int32)
    tile_e = jnp.minimum(jnp.searchsorted(tile_end, jnp.arange(n_tiles), side="right"),
                         N_EXPERTS - 1).astype(jnp.int32)
    tok_id = jnp.arange(2 * t, dtype=jnp.int32) // 2
    src = jnp.zeros((n_tiles * tm,), jnp.int32).at[pos].set(tok_id)
    row_w = jnp.zeros((n_tiles * tm,), F32).at[pos].set(wt.reshape(-1))
    pos_t = pos.reshape(t // tm, tm, 2).transpose(0, 2, 1).astype(jnp.int32)
    return tile_e, n_used, src.reshape(n_tiles, 1, tm), row_w.reshape(-1, 1), pos_t


def _moe(tok, h, x, w_rt, b_rt, w1, w3, w2, modr, layer, final_gain, final):
    eid, wt = _router(h, w_rt, b_rt)
    tile_e, n_used, src, row_w, pos = _routing_tables(eid, wt, MOE_TILE)
    ys = _moe_experts(h, tile_e, n_used, src, row_w, w1, w3, w2)
    return _moe_combine(tok, ys, pos, x, modr, layer, final_gain, final)


def _pick_tile(n, candidates):
    for c in candidates:
        if n % c == 0:
            return c
    raise ValueError(f"no tile for {n}")


def kernel(x_prompt, x_sample, cache_attn_k, cache_attn_v, state_ret, state_delta, c, c_ctx, w_mod, b_mod, norm1, norm2, w_in_even, diff_lambda, subln_gain, ret_decay, ret_norm, w_out_even, w_in_odd, conv_w, dn_a_log, dn_dt_bias, dn_norm, w_out_odd, moe_w_group, moe_b_group, moe_w_router, moe_b_router, moe_w1, moe_w3, moe_w2, final_norm):
    n_ctx, l_ctx, d = x_prompt.shape
    n_lat, l_lat, _ = x_sample.shape
    depth = w_mod.shape[0]
    tok = _Tokens(n_ctx, l_ctx, n_lat, l_lat)
    assert 1 + n_lat <= MAX_CONDS and l_ctx % ROW_TILE == 0 and l_lat % ROW_TILE == 0
    h_a = w_in_even.shape[2] // (7 * HEAD)
    h_c = dn_a_log.shape[2]
    d_c = h_c * HEAD

    x = jnp.concatenate([x_prompt.reshape(-1, d), x_sample.reshape(-1, d)], axis=0)
    cond = jnp.zeros((MAX_CONDS, d), F32).at[0].set(c_ctx).at[1:1 + n_lat].set(c)
    mod = _modulation(cond, w_mod, b_mod)
    modr = mod.reshape(depth, MAX_CONDS, 6, d).transpose(0, 2, 1, 3).reshape(depth * 6 * MAX_CONDS, 1, d)

    rt_rows = 4 * SUBLANES
    w_rt = jnp.concatenate([moe_w_group, moe_w_router], axis=2).transpose(0, 2, 1)
    w_rt = jnp.pad(w_rt, ((0, 0), (0, rt_rows - w_rt.shape[1]), (0, 0)))
    b_rt = jnp.pad(jnp.concatenate([moe_b_group, moe_b_router], axis=1),
                   ((0, 0), (0, rt_rows - N_GROUPS - N_EXPERTS)))[:, :, None]

    ks, vs, rets, deltas = [], [], [], []
    y = None
    for layer in range(depth):
        i = layer // 2
        if layer % 2 == 0:
            w_in = w_in_even[i].astype(BF16)
            p = _in_proj(tok, x, norm1[layer], modr, layer, w_in, 512, _pick_tile(w_in.shape[1], (1792, 1024, 512)))
            prm = (diff_lambda[i], subln_gain[i], ret_norm[i], ret_decay[i])
            oa_c, or_c, k_new, v_new, s_new = _even_ctx(tok, p, layer, *prm)
            oa_l, or_l = _even_lat(tok, p, layer, i, *prm, cache_attn_k, cache_attn_v, state_ret)
            ks.append(k_new)
            vs.append(v_new)
            rets.append(s_new)
            outs = [jnp.concatenate([oa_c, oa_l], axis=0), jnp.concatenate([or_c, or_l], axis=0)]
            w_out = w_out_even[i].astype(BF16)
            ws = [w_out[:h_a * HEAD], w_out[h_a * HEAD:]]
        else:
            w_in = w_in_odd[i]
            n_main = 4 * d_c
            p = _in_proj(tok, x, norm1[layer], modr, layer, w_in[:, :n_main].astype(BF16), 512, 1024)
            w_ab = jnp.pad(w_in[:, n_main:], ((0, 0), (0, LANES - 4 * h_c))).astype(BF16)
            ab = _in_proj(tok, x, norm1[layer], modr, layer, w_ab, 512, LANES)
            gc, gr = _delta_gate_layouts(ab, h_c)
            par = jnp.broadcast_to(jnp.concatenate([dn_a_log[i], dn_dt_bias[i]], axis=0).T[:, :, None],
                                   (h_c, 4, LANES))
            o_c, s_new = _delta(tok, p, gc, gr, par, conv_w[i], dn_norm[i], True)
            o_l, = _delta(tok, p, gc, gr, par, conv_w[i], dn_norm[i], False, state_delta, i)
            deltas.append(s_new)
            outs = [jnp.concatenate([o_c, o_l], axis=0)]
            ws = [w_out_odd[i].astype(BF16)]
        x, h2 = _out_proj(tok, outs, ws, x, norm2[layer], modr, layer)
        final = layer == depth - 1
        res = _moe(tok, h2, x, w_rt[layer], b_rt[layer], moe_w1[layer].astype(BF16),
                   moe_w3[layer].astype(BF16), moe_w2[layer].astype(BF16), modr, layer, final_norm, final)
        x = res[0]
        if final:
            y = res[1]

    y_prompt = y[:tok.t_ctx].reshape(n_ctx, l_ctx, d)
    y_sample = y[tok.t_ctx:].reshape(n_lat, l_lat, d)
    return (y_prompt, y_sample, jnp.concatenate(ks, axis=1), jnp.concatenate(vs, axis=1),
            jnp.concatenate(rets, axis=1), jnp.concatenate(deltas, axis=1))
```

```python
import functools
import math

import jax
import jax.numpy as jnp
from jax import lax
from jax.experimental import pallas as pl
from jax.experimental.pallas import tpu as pltpu

F32 = jnp.float32
BF16 = jnp.bfloat16
HI = lax.Precision.HIGHEST

GRID_W = 64
DA = 64
HEAD = 128
CONV_K = 5
CHUNK = 64
ROPE_BASE = 10000.0
N_GROUPS = 4
E_PER_GROUP = 4
N_EXPERTS = N_GROUPS * E_PER_GROUP
EPS = 1e-6

LANES = 128
SUBLANES = 8
MAX_CONDS = SUBLANES
ROW_TILE = 256
MOE_TILE = 256
VMEM_LIMIT = 56 * 1024 * 1024


def _cparams(sem):
    return pltpu.CompilerParams(dimension_semantics=sem, vmem_limit_bytes=VMEM_LIMIT)


def _bdot(a, b):
    return jnp.dot(a.astype(BF16), b.astype(BF16), preferred_element_type=F32)


def _bdot_nt(a, b):
    return lax.dot_general(a.astype(BF16), b.astype(BF16), (((1,), (1,)), ((), ())),
                           preferred_element_type=F32)


def _bdot_tn(a, b):
    return lax.dot_general(a.astype(BF16), b.astype(BF16), (((0,), (0,)), ((), ())),
                           preferred_element_type=F32)


def _hdot(a, b):
    return jnp.dot(a, b, precision=HI, preferred_element_type=F32)


def _idot(a, b):
    return _bdot(a, b)


def _sigmoid(x):
    return 1.0 / (1.0 + jnp.exp(-x))


def _silu(x):
    return x * _sigmoid(x)


def _softplus(x):
    return jnp.maximum(x, 0.0) + jnp.log(1.0 + jnp.exp(-jnp.abs(x)))


def _rms(x, gain):
    return x * lax.rsqrt(jnp.mean(x * x, axis=-1, keepdims=True) + EPS) * gain


def _softmax(s):
    m = jnp.max(s, axis=-1, keepdims=True)
    e = jnp.exp(s - m)
    return e / jnp.sum(e, axis=-1, keepdims=True)


def _mod_kernel(c_ref, w_ref, b_ref, o_ref):
    c = c_ref[...]
    o_ref[0] = _bdot(_silu(c), w_ref[0]) + b_ref[0]


def _modulation(cond, w_mod, b_mod, tn=1024):
    depth, d, n = w_mod.shape
    return pl.pallas_call(
        _mod_kernel,
        grid=(depth, n // tn),
        in_specs=[pl.BlockSpec((MAX_CONDS, d), lambda l, j: (0, 0)),
                  pl.BlockSpec((1, d, tn), lambda l, j: (l, 0, j)),
                  pl.BlockSpec((1, 1, tn), lambda l, j: (l, 0, j))],
        out_specs=pl.BlockSpec((1, MAX_CONDS, tn), lambda l, j: (l, 0, j)),
        out_shape=jax.ShapeDtypeStruct((depth, MAX_CONDS, n), F32),
        compiler_params=_cparams(("parallel", "parallel")),
        name="adaln_mod",
    )(cond, w_mod, b_mod.reshape(depth, 1, n))


class _Tokens:
    def __init__(self, n_ctx, l_ctx, n_lat, l_lat):
        self.n_ctx, self.l_ctx, self.n_lat, self.l_lat = n_ctx, l_ctx, n_lat, l_lat
        self.t_ctx = n_ctx * l_ctx
        self.t = self.t_ctx + n_lat * l_lat

    def cond_row(self, i, tm):
        n0 = self.t_ctx // tm
        return jnp.where(i < n0, 0, 1 + (i - n0) // (self.l_lat // tm))


def _mod_spec(tok, tm, layer, which, d):
    base = (layer * 6 + which) * MAX_CONDS
    return pl.BlockSpec((1, 1, d), lambda i, *_: (base + tok.cond_row(i, tm), 0, 0))


def _inproj_kernel(x_ref, g_ref, sh_ref, sc_ref, w_ref, o_ref, h_scr):
    @pl.when(pl.program_id(1) == 0)
    def _():
        h = _rms(x_ref[...], g_ref[...]) * (1.0 + sc_ref[0]) + sh_ref[0]
        h_scr[...] = h.astype(BF16)

    o_ref[...] = jnp.dot(h_scr[...], w_ref[...], preferred_element_type=F32)


def _in_proj(tok, x, gain, modr, layer, w, tm, tn):
    t, d = x.shape
    n = w.shape[1]
    return pl.pallas_call(
        _inproj_kernel,
        grid=(t // tm, n // tn),
        in_specs=[pl.BlockSpec((tm, d), lambda i, j: (i, 0)),
                  pl.BlockSpec((1, d), lambda i, j: (0, 0)),
                  _mod_spec(tok, tm, layer, 0, d),
                  _mod_spec(tok, tm, layer, 1, d),
                  pl.BlockSpec((d, tn), lambda i, j: (0, j))],
        out_specs=pl.BlockSpec((tm, tn), lambda i, j: (i, j)),
        out_shape=jax.ShapeDtypeStruct((t, n), F32),
        scratch_shapes=[pltpu.VMEM((tm, d), BF16)],
        compiler_params=_cparams(("parallel", "arbitrary")),
        name="in_proj",
    )(x, gain.reshape(1, d), modr, modr, w)


def _outproj_kernel(n_in, *refs):
    o_refs = refs[:n_in]
    w_refs = refs[n_in:2 * n_in]
    x_ref, gate_ref, g2_ref, sh_ref, sc_ref, xo_ref, h_ref = refs[2 * n_in:]
    acc = _bdot(o_refs[0][...], w_refs[0][...])
    for o_r, w_r in zip(o_refs[1:], w_refs[1:]):
        acc = acc + _bdot(o_r[...], w_r[...])
    x = x_ref[...] + gate_ref[0] * acc
    xo_ref[...] = x
    h = _rms(x, g2_ref[...]) * (1.0 + sc_ref[0]) + sh_ref[0]
    h_ref[...] = h


def _out_proj(tok, outs, ws, x, gain2, modr, layer, tm=ROW_TILE):
    t, d = x.shape
    n_in = len(outs)
    in_specs = ([pl.BlockSpec((tm, o.shape[1]), lambda i: (i, 0)) for o in outs]
                + [pl.BlockSpec(w.shape, lambda i: (0, 0)) for w in ws]
                + [pl.BlockSpec((tm, d), lambda i: (i, 0)),
                   _mod_spec(tok, tm, layer, 2, d),
                   pl.BlockSpec((1, d), lambda i: (0, 0)),
                   _mod_spec(tok, tm, layer, 3, d),
                   _mod_spec(tok, tm, layer, 4, d)])
    return pl.pallas_call(
        functools.partial(_outproj_kernel, n_in),
        grid=(t // tm,),
        in_specs=in_specs,
        out_specs=[pl.BlockSpec((tm, d), lambda i: (i, 0)),
                   pl.BlockSpec((tm, d), lambda i: (i, 0))],
        out_shape=[jax.ShapeDtypeStruct((t, d), F32), jax.ShapeDtypeStruct((t, d), F32)],
        compiler_params=_cparams(("parallel",)),
        name="out_proj",
    )(*outs, *ws, x, modr, gain2.reshape(1, d), modr, modr)


def _lambda(dl_ref, lam_init):
    dl = dl_ref[0]
    s1 = jnp.sum(dl[0:1] * dl[1:2], axis=-1, keepdims=True)
    s2 = jnp.sum(dl[2:3] * dl[3:4], axis=-1, keepdims=True)
    return jnp.exp(s1) - jnp.exp(s2) + lam_init


def _log_gammas(rdec_ref):
    lg = -_softplus(-rdec_ref[0])
    return lg[0:1, 0:1], lg[1:2, 0:1]


def _diff_attn(q, k1, k2, v, lam):
    p1 = _softmax(_bdot_nt(q[:, :DA], k1) * (DA ** -0.5))
    p2 = _softmax(_bdot_nt(q[:, DA:], k2) * (DA ** -0.5))
    return _bdot(p1 - lam * p2, v)


def _ret_mask(row0, nq, nk, lg_f, lg_b):
    qi = lax.broadcasted_iota(jnp.int32, (nq, nk), 0) + row0
    kj = lax.broadcasted_iota(jnp.int32, (nq, nk), 1)
    diff = (qi - kj).astype(F32)
    fwd = jnp.where(diff >= 0, jnp.exp(jnp.maximum(diff, 0.0) * lg_f), 0.0)
    bwd = jnp.where(diff <= 0, jnp.exp(jnp.maximum(-diff, 0.0) * lg_b), 0.0)
    return fwd + bwd


def _even_ctx_kernel(lam_init, qa_ref, ka_ref, va_ref, qb_ref, kb_ref, vb_ref, gb_ref,
                     dl_ref, subln_ref, retn_ref, rdec_ref,
                     oa_ref, or_ref, ko_ref, vo_ref, so_ref):
    n = qa_ref.shape[0]
    lam = _lambda(dl_ref, lam_init)
    ka = ka_ref[...]
    va = va_ref[...]
    ko_ref[0, 0, 0] = ka
    vo_ref[0, 0, 0] = va
    kab = ka.astype(BF16)
    o = _diff_attn(qa_ref[...], kab[:, :DA], kab[:, DA:], va.astype(BF16), lam)
    oa_ref[...] = _rms(o, subln_ref[...]) * (1.0 - lam_init)

    lg_f, lg_b = _log_gammas(rdec_ref)
    kb = kb_ref[...] * (HEAD ** -0.5)
    vb = vb_ref[...].astype(BF16)
    a = _bdot_nt(qb_ref[...], kb) * _ret_mask(0, n, n, lg_f, lg_b)
    o = _bdot(a, vb)
    or_ref[...] = _rms(o, retn_ref[...]) * _silu(gb_ref[...])
    pos = lax.broadcasted_iota(jnp.int32, (n, 1), 0).astype(F32)
    so_ref[0, 0, 0, 0] = _bdot_tn(kb * jnp.exp((n - 1.0 - pos) * lg_f), vb)
    so_ref[0, 0, 1, 0] = _bdot_tn(kb * jnp.exp(pos * lg_b), vb)


def _head_spec(rows, col0, row0=0):
    return pl.BlockSpec((rows, HEAD), lambda s, h: (row0 + s, col0 + h))


def _even_param_specs():
    return [pl.BlockSpec((1, 4, DA), lambda s, h: (0, 0, 0)),
            pl.BlockSpec((1, HEAD), lambda s, h: (0, 0)),
            pl.BlockSpec((1, HEAD), lambda s, h: (0, 0)),
            pl.BlockSpec((1, 2, LANES), lambda s, h: (h, 0, 0))]


def _even_params(dl, subln, retn, rdec):
    n_heads = rdec.shape[1]
    rdec_b = jnp.broadcast_to(rdec.T[:, :, None], (n_heads, 2, LANES))
    return dl.reshape(1, 4, DA), subln.reshape(1, HEAD), retn.reshape(1, HEAD), rdec_b


def _even_ctx(tok, p, layer, dl, subln, retn, rdec):
    n_heads = p.shape[1] // (7 * HEAD)
    lam_init = 0.8 - 0.6 * math.exp(-0.3 * layer)
    l, nseq = tok.l_ctx, tok.n_ctx
    dh = n_heads * HEAD
    return pl.pallas_call(
        functools.partial(_even_ctx_kernel, lam_init),
        grid=(nseq, n_heads),
        in_specs=[_head_spec(l, c * n_heads) for c in range(7)] + _even_param_specs(),
        out_specs=[pl.BlockSpec((l, HEAD), lambda s, h: (s, h)),
                   pl.BlockSpec((l, HEAD), lambda s, h: (s, h)),
                   pl.BlockSpec((1, 1, 1, l, HEAD), lambda s, h: (s, 0, h, 0, 0)),
                   pl.BlockSpec((1, 1, 1, l, HEAD), lambda s, h: (s, 0, h, 0, 0)),
                   pl.BlockSpec((1, 1, 2, 1, HEAD, HEAD), lambda s, h: (s, 0, 0, h, 0, 0))],
        out_shape=[jax.ShapeDtypeStruct((tok.t_ctx, dh), F32),
                   jax.ShapeDtypeStruct((tok.t_ctx, dh), F32),
                   jax.ShapeDtypeStruct((nseq, 1, n_heads, l, HEAD), F32),
                   jax.ShapeDtypeStruct((nseq, 1, n_heads, l, HEAD), F32),
                   jax.ShapeDtypeStruct((nseq, 1, 2, n_heads, HEAD, HEAD), F32)],
        compiler_params=_cparams(("parallel", "parallel")),
        name="even_mixer_ctx",
    )(*([p] * 7), *_even_params(dl, subln, retn, rdec))


def _rope(x, cos, sin_signed):
    half = DA // 2
    lane = lax.broadcasted_iota(jnp.int32, x.shape, 1)
    swapped = jnp.where((lane % DA) < half,
                        pltpu.roll(x, x.shape[1] - half, axis=1),
                        pltpu.roll(x, half, axis=1))
    return x * cos + swapped * sin_signed


def _even_lat_kernel(lam_init, tq, qa_ref, ka_ref, va_ref, qb_ref, kb_ref, vb_ref, gb_ref,
                     ck_ref, cv_ref, s0_ref, cos_ref, sin_ref,
                     dl_ref, subln_ref, retn_ref, rdec_ref,
                     oa_ref, or_ref, kall, vall):
    n = qa_ref.shape[0]
    past = ck_ref.shape[3]
    lam = _lambda(dl_ref, lam_init)
    cos = cos_ref[...]
    sin = sin_ref[...]
    kall[0:past, :] = ck_ref[0, 0, 0].astype(BF16)
    kall[past:past + n, :] = _rope(ka_ref[...], cos, sin).astype(BF16)
    vall[0:past, :] = cv_ref[0, 0, 0].astype(BF16)
    vall[past:past + n, :] = va_ref[...].astype(BF16)

    lg_f, lg_b = _log_gammas(rdec_ref)
    kb = (kb_ref[...] * (HEAD ** -0.5)).astype(BF16)
    vb = vb_ref[...].astype(BF16)
    s0f = s0_ref[0, 0, 0, 0].astype(BF16)
    s0b = s0_ref[0, 0, 1, 0].astype(BF16)
    for blk in range(n // tq):
        rows = pl.ds(blk * tq, tq)
        q = _rope(qa_ref[rows, :], cos[blk * tq:(blk + 1) * tq], sin[blk * tq:(blk + 1) * tq])
        o = _diff_attn(q, kall[:, :DA], kall[:, DA:], vall[...], lam)
        oa_ref[rows, :] = _rms(o, subln_ref[...]) * (1.0 - lam_init)

        qb = qb_ref[rows, :]
        a = _bdot_nt(qb, kb) * _ret_mask(blk * tq, tq, n, lg_f, lg_b)
        pos = (lax.broadcasted_iota(jnp.int32, (tq, 1), 0) + blk * tq).astype(F32)
        o = (_bdot(a, vb) + _bdot(qb * jnp.exp((pos + 1.0) * lg_f), s0f)
             + _bdot(qb * jnp.exp((n - pos) * lg_b), s0b))
        or_ref[rows, :] = _rms(o, retn_ref[...]) * _silu(gb_ref[rows, :])


def _rope_tables(n_tok):
    n_freq = DA // 4
    t = jnp.arange(n_tok)
    inv_freq = ROPE_BASE ** (-jnp.arange(n_freq, dtype=F32) / n_freq)
    ang = jnp.concatenate([(t // GRID_W).astype(F32)[:, None] * inv_freq,
                           (t % GRID_W).astype(F32)[:, None] * inv_freq], axis=-1)
    cos, sin = jnp.cos(ang), jnp.sin(ang)
    return jnp.tile(cos, (1, 4)), jnp.tile(jnp.concatenate([-sin, sin], axis=-1), (1, 2))


def _even_lat(tok, p, layer, i, dl, subln, retn, rdec, cache_k, cache_v, state_ret, tq=256):
    n_heads = p.shape[1] // (7 * HEAD)
    lam_init = 0.8 - 0.6 * math.exp(-0.3 * layer)
    l, nseq = tok.l_lat, tok.n_lat
    row0 = tok.t_ctx // l
    past = cache_k.shape[3]
    dh = n_heads * HEAD
    cos, sin = _rope_tables(l)
    return pl.pallas_call(
        functools.partial(_even_lat_kernel, lam_init, tq),
        grid=(nseq, n_heads),
        in_specs=[_head_spec(l, c * n_heads, row0) for c in range(7)]
        + [pl.BlockSpec((1, 1, 1, past, HEAD), lambda s, h: (s, i, h, 0, 0)),
           pl.BlockSpec((1, 1, 1, past, HEAD), lambda s, h: (s, i, h, 0, 0)),
           pl.BlockSpec((1, 1, 2, 1, HEAD, HEAD), lambda s, h: (s, i, 0, h, 0, 0)),
           pl.BlockSpec((l, HEAD), lambda s, h: (0, 0)),
           pl.BlockSpec((l, HEAD), lambda s, h: (0, 0))]
        + _even_param_specs(),
        out_specs=[pl.BlockSpec((l, HEAD), lambda s, h: (s, h)),
                   pl.BlockSpec((l, HEAD), lambda s, h: (s, h))],
        out_shape=[jax.ShapeDtypeStruct((nseq * l, dh), F32),
                   jax.ShapeDtypeStruct((nseq * l, dh), F32)],
        scratch_shapes=[pltpu.VMEM((past + l, HEAD), BF16), pltpu.VMEM((past + l, HEAD), BF16)],
        compiler_params=_cparams(("parallel", "parallel")),
        name="even_mixer_lat",
    )(*([p] * 7), cache_k, cache_v, state_ret, cos, sin, *_even_params(dl, subln, retn, rdec))


def _conv_silu(x, w):
    n = x.shape[0]
    pad = (CONV_K - 1) // 2
    row = lax.broadcasted_iota(jnp.int32, x.shape, 0)
    acc = x * w[pad:pad + 1]
    for j in range(CONV_K):
        d = j - pad
        if d == 0:
            continue
        shifted = pltpu.roll(x, (-d) % n, axis=0)
        valid = (row + d >= 0) & (row + d < n)
        acc = acc + jnp.where(valid, shifted, 0.0) * w[j:j + 1]
    return _silu(acc)


def _l2n(x):
    return x * lax.rsqrt(jnp.sum(x * x, axis=-1, keepdims=True) + EPS)


TRI_BLOCK = 16


def _unit_tri_solves(lms, rhss):
    c = lms[0].shape[0]
    ri = lax.broadcasted_iota(jnp.int32, (c, c), 0)
    ci = lax.broadcasted_iota(jnp.int32, (c, c), 1)
    blk = TRI_BLOCK
    same = (ri // blk) == (ci // blk)
    pws = [jnp.where(same, -lm, 0.0) for lm in lms]
    ns = list(pws)
    span = 1
    while 2 * span < blk:
        pws = [_idot(pw, pw) for pw in pws]
        ns = [n + pw + _idot(n, pw) for n, pw in zip(ns, pws)]
        span *= 2
    while blk < c:
        pair = ((ri // (2 * blk)) == (ci // (2 * blk))) & ((ri // blk) != (ci // blk))
        cs = [jnp.where(pair, lm, 0.0) for lm in lms]
        xs = [cm + _idot(n, cm) for n, cm in zip(ns, cs)]
        ns = [n - (x + _idot(x, n)) for n, x in zip(ns, xs)]
        blk *= 2
    return [rhs + _idot(n, rhs) for n, rhs in zip(ns, rhss)]


DELTA_UNIT = 4


def _delta_kernel(has_s0, q_ref, k_ref, v_ref, z_ref, wq_ref, wk_ref, wv_ref,
                  gc_ref, gr_ref, par_ref, gain_ref, *rest):
    if has_s0:
        s0_ref, o_ref = rest[:2]
        so_ref = None
    else:
        o_ref, so_ref = rest[:2]
        s0_ref = None
    qs, ks, vs, gcol, grow, b_s, w_s, q_s, o_s, e_s, st_s = rest[2:]
    n = q_ref.shape[0]
    n_chunks = n // CHUNK
    c = CHUNK

    qs[...] = _l2n(_conv_silu(q_ref[...], wq_ref[...])) * (HEAD ** -0.5)
    ks[...] = _l2n(_conv_silu(k_ref[...], wk_ref[...]))
    vs[...] = _conv_silu(v_ref[...], wv_ref[...])

    par = par_ref[0]
    neg_a_f, neg_a_b = -jnp.exp(par[0:1, 0:1]), -jnp.exp(par[1:2, 0:1])
    dt_f, dt_b = par[2:3, 0:1], par[3:4, 0:1]

    def gates(x, idx):
        g = jnp.where(idx == 0, neg_a_f, neg_a_b) * _softplus(x + jnp.where(idx == 0, dt_f, dt_b))
        return jnp.where(idx < 2, g, _sigmoid(x))

    xc = gc_ref[0]
    gcol[...] = gates(xc, lax.broadcasted_iota(jnp.int32, xc.shape, 1))
    xr = gr_ref[0]
    grow[...] = gates(xr, lax.broadcasted_iota(jnp.int32, xr.shape, 1))

    ri = lax.broadcasted_iota(jnp.int32, (c, c), 0)
    ci = lax.broadcasted_iota(jnp.int32, (c, c), 1)
    masks = [(ri >= ci, ri > ci), (ri <= ci, ri < ci)]

    unit = min(DELTA_UNIT, n_chunks)

    def phase_a(ui, carry):
        probs = []
        for cc in range(unit):
            cidx = ui * unit + cc
            rows = pl.ds(pl.multiple_of(cidx * c, c), c)
            qc, kc, vc = qs[rows, :], ks[rows, :], vs[rows, :]
            kcb = kc.astype(BF16)
            kk = _bdot_nt(kcb, kcb)
            qkr = _bdot_nt(qc, kcb)
            gcl = gcol[rows, :]
            grw = grow[cidx]
            for d in range(2):
                incl, strict = masks[d]
                incl_t = masks[1 - d][0]
                g_c, b_c = gcl[:, d:d + 1], gcl[:, 2 + d:3 + d]
                g_r = grw[d:d + 1, :]
                gcc = jnp.sum(jnp.where(incl, g_r, 0.0), axis=1, keepdims=True)
                gcr = jnp.sum(jnp.where(incl_t, g_c, 0.0), axis=0, keepdims=True)
                decay = jnp.where(incl, jnp.exp(jnp.where(incl, gcc - gcr, 0.0)), 0.0)
                egc = jnp.exp(gcc)
                g_last = jnp.sum(g_c, axis=0, keepdims=True)
                probs.append(dict(
                    d=d, cidx=cidx,
                    lm=jnp.where(strict, b_c * kk * decay, 0.0),
                    rhs=jnp.concatenate([vc * b_c, kc * (b_c * egc)], axis=-1),
                    qk=jnp.where(incl, qkr * decay, 0.0).astype(BF16),
                    kexp=(kc * jnp.exp(g_last - gcc)).astype(BF16),
                    qexp=qc * egc,
                    eg=jnp.broadcast_to(jnp.exp(g_last), (1, HEAD))))
        sols = _unit_tri_solves([p["lm"] for p in probs], [p["rhs"] for p in probs])
        sols = [s.astype(BF16) for s in sols]
        kts = [_bdot_tn(p["kexp"], s) for p, s in zip(probs, sols)]
        qos = [_bdot(p["qk"], s) for p, s in zip(probs, sols)]
        for p, kt, qo in zip(probs, kts, qos):
            d, cidx = p["d"], p["cidx"]
            b_s[d, cidx] = kt[:, :HEAD]
            w_s[d, cidx] = kt[:, HEAD:].astype(BF16)
            o_s[d, cidx] = qo[:, :HEAD]
            q_s[d, cidx] = (p["qexp"] - qo[:, HEAD:]).astype(BF16)
            e_s[d, cidx] = p["eg"]
        return carry

    lax.fori_loop(0, n_chunks // unit, phase_a, 0)

    def phase_b(step, carry):
        nxt = []
        for d, s in enumerate(carry):
            cidx = step if d == 0 else n_chunks - 1 - step
            sb = s.astype(BF16)
            st_s[d, cidx] = sb
            nxt.append(e_s[d, cidx] * s + b_s[d, cidx] - jnp.dot(w_s[d, cidx], sb, preferred_element_type=F32))
        return tuple(nxt)

    if has_s0:
        s_init = (s0_ref[0, 0, 0, 0], s0_ref[0, 0, 1, 0])
    else:
        s_init = (jnp.zeros((HEAD, HEAD), F32), jnp.zeros((HEAD, HEAD), F32))
    s_fin = lax.fori_loop(0, n_chunks, phase_b, s_init, unroll=n_chunks <= DELTA_UNIT)
    if so_ref is not None:
        so_ref[0, 0, 0, 0] = s_fin[0]
        so_ref[0, 0, 1, 0] = s_fin[1]

    gain = gain_ref[...]

    def phase_c(ui, carry):
        cids = [ui * unit + cc for cc in range(unit)]
        prods = [[jnp.dot(q_s[d, cidx], st_s[d, cidx], preferred_element_type=F32) for d in range(2)]
                 for cidx in cids]
        for cidx, (of, ob) in zip(cids, prods):
            rows = pl.ds(pl.multiple_of(cidx * c, c), c)
            o = (of + o_s[0, cidx]) + (ob + o_s[1, cidx])
            o_ref[rows, :] = _rms(o, gain) * _silu(z_ref[rows, :])
        return carry

    lax.fori_loop(0, n_chunks // unit, phase_c, 0)


def _delta(tok, p, gc, gr, par, conv_w, gain, ctx, state=None, j=0):
    n_heads = par.shape[0]
    dc = n_heads * HEAD
    l, nseq = (tok.l_ctx, tok.n_ctx) if ctx else (tok.l_lat, tok.n_lat)
    row0 = 0 if ctx else tok.t_ctx // l
    nc = l // CHUNK
    in_specs = ([_head_spec(l, c * n_heads, row0) for c in range(4)]
                + [pl.BlockSpec((CONV_K, HEAD), lambda s, h, c=c: (0, c * n_heads + h)) for c in range(3)]
                + [pl.BlockSpec((1, l, SUBLANES), lambda s, h: (h, row0 + s, 0)),
                   pl.BlockSpec((1, l // CHUNK, SUBLANES, CHUNK), lambda s, h: (h, row0 + s, 0, 0)),
                   pl.BlockSpec((1, 4, LANES), lambda s, h: (h, 0, 0)),
                   pl.BlockSpec((1, HEAD), lambda s, h: (0, 0))])
    args = [p, p, p, p, conv_w, conv_w, conv_w, gc, gr, par, gain.reshape(1, HEAD)]
    out_specs = [pl.BlockSpec((l, HEAD), lambda s, h: (s, h))]
    out_shape = [jax.ShapeDtypeStruct((nseq * l, dc), F32)]
    if ctx:
        out_specs.append(pl.BlockSpec((1, 1, 2, 1, HEAD, HEAD), lambda s, h: (s, 0, 0, h, 0, 0)))
        out_shape.append(jax.ShapeDtypeStruct((nseq, 1, 2, n_heads, HEAD, HEAD), F32))
    else:
        in_specs.append(pl.BlockSpec((1, 1, 2, 1, HEAD, HEAD), lambda s, h: (s, j, 0, h, 0, 0)))
        args.append(state)
    return pl.pallas_call(
        functools.partial(_delta_kernel, not ctx),
        grid=(nseq, n_heads),
        in_specs=in_specs,
        out_specs=out_specs,
        out_shape=out_shape,
        scratch_shapes=[pltpu.VMEM((l, HEAD), F32), pltpu.VMEM((l, HEAD), F32), pltpu.VMEM((l, HEAD), F32),
                        pltpu.VMEM((l, SUBLANES), F32), pltpu.VMEM((nc, SUBLANES, CHUNK), F32),
                        pltpu.VMEM((2, nc, HEAD, HEAD), F32), pltpu.VMEM((2, nc, HEAD, HEAD), BF16),
                        pltpu.VMEM((2, nc, CHUNK, HEAD), BF16), pltpu.VMEM((2, nc, CHUNK, HEAD), F32),
                        pltpu.VMEM((2, nc, 1, HEAD), F32), pltpu.VMEM((2, nc, HEAD, HEAD), BF16)],
        compiler_params=_cparams(("parallel", "parallel")),
        name="delta_ctx" if ctx else "delta_lat",
    )(*args)


def _delta_gate_layouts(ab, n_heads):
    t = ab.shape[0]
    x = ab[:, :4 * n_heads].reshape(t, 4, n_heads)
    x = jnp.pad(x, ((0, 0), (0, SUBLANES - 4), (0, 0)))
    gc = x.transpose(2, 0, 1)
    gr = x.reshape(t // CHUNK, CHUNK, SUBLANES, n_heads).transpose(3, 0, 2, 1)
    return gc, gr


def _router_kernel(h_ref, w_ref, b_ref, eid_ref, wt_ref, rank_ref, cnt_ref, tri_scr, run_scr):
    i = pl.program_id(0)
    tm = h_ref.shape[0]
    na = 2 * tm

    @pl.when(i == 0)
    def _():
        run_scr[...] = jnp.zeros_like(run_scr)
        a0 = lax.broadcasted_iota(jnp.int32, (na, na), 0)
        a1 = lax.broadcasted_iota(jnp.int32, (na, na), 1)
        tri_scr[...] = (a0 <= a1).astype(BF16)

    h = h_ref[...]
    w = w_ref[...]
    h_hi = h.astype(BF16)
    h_lo = (h - h_hi.astype(F32)).astype(BF16)
    w_hi = w.astype(BF16)
    w_lo = (w - w_hi.astype(F32)).astype(BF16)
    lt = _bdot_nt(w_hi, h_hi) + (_bdot_nt(w_lo, h_hi) + _bdot_nt(w_hi, h_lo)) + b_ref[...]

    def first_argmax(x, valid=None):
        rows = lax.broadcasted_iota(jnp.int32, x.shape, 0)
        if valid is not None:
            x = jnp.where(valid, x, -jnp.inf)
        m = jnp.max(x, axis=0, keepdims=True)
        idx = jnp.min(jnp.where(x == m, rows, x.shape[0]), axis=0, keepdims=True)
        return m, idx

    gl = lt[0:N_GROUPS]
    gmax, gidx = first_argmax(gl)
    g_w = 1.0 / jnp.sum(jnp.exp(gl - gmax), axis=0, keepdims=True)
    el = jnp.zeros((E_PER_GROUP, tm), F32)
    for g in range(N_GROUPS):
        lo = N_GROUPS + g * E_PER_GROUP
        el = jnp.where(gidx == g, lt[lo:lo + E_PER_GROUP], el)
    v1, i1 = first_argmax(el)
    rows = lax.broadcasted_iota(jnp.int32, el.shape, 0)
    v2, i2 = first_argmax(el, rows != i1)
    e21 = jnp.exp(v2 - v1)
    p1 = 1.0 / (1.0 + e21)
    e1 = gidx * E_PER_GROUP + i1
    e2 = gidx * E_PER_GROUP + i2
    eid_ref[...] = jnp.concatenate([e1, e2], axis=0)[None]
    wt_ref[...] = jnp.concatenate([g_w * p1, g_w * (e21 * p1)], axis=0)[None]

    e_all = jnp.concatenate([e1, e2], axis=1)
    onehot = lax.broadcasted_iota(jnp.int32, (N_EXPERTS, na), 0) == e_all
    csum = jnp.dot(onehot.astype(BF16), tri_scr[...], preferred_element_type=F32)
    run = run_scr[:, 0:1]
    rank = jnp.sum(jnp.where(onehot, csum + run, 0.0), axis=0, keepdims=True) - 1.0
    rank = rank.astype(jnp.int32)
    rank_ref[...] = jnp.concatenate([rank[:, :tm], rank[:, tm:]], axis=0)[None]
    run_new = run_scr[...] + csum[:, na - 1:na]
    run_scr[...] = run_new
    cnt_ref[...] = run_new


def _router(h, w_rt, b_rt, tm=ROW_TILE):
    t, d = h.shape
    nt = t // tm
    nr = w_rt.shape[0]
    tile_spec = pl.BlockSpec((1, 2, tm), lambda i: (i, 0, 0))
    return pl.pallas_call(
        _router_kernel,
        grid=(nt,),
        in_specs=[pl.BlockSpec((tm, d), lambda i: (i, 0)),
                  pl.BlockSpec((nr, d), lambda i: (0, 0)),
                  pl.BlockSpec((nr, 1), lambda i: (0, 0))],
        out_specs=[tile_spec, tile_spec, tile_spec,
                   pl.BlockSpec((N_EXPERTS, LANES), lambda i: (0, 0))],
        out_shape=[jax.ShapeDtypeStruct((nt, 2, tm), jnp.int32),
                   jax.ShapeDtypeStruct((nt, 2, tm), F32),
                   jax.ShapeDtypeStruct((nt, 2, tm), jnp.int32),
                   jax.ShapeDtypeStruct((N_EXPERTS, LANES), F32)],
        scratch_shapes=[pltpu.VMEM((2 * tm, 2 * tm), BF16), pltpu.VMEM((N_EXPERTS, LANES), F32)],
        compiler_params=_cparams(("arbitrary",)),
        name="moe_router",
    )(h, w_rt, b_rt)


def _dispatch_kernel(pos_ref, h_ref, xs_in, xs_out, sem):
    del xs_in
    tm = h_ref.shape[0]

    def row_copy(k, r):
        return pltpu.make_async_copy(h_ref.at[pl.ds(r, 1), :], xs_out.at[pl.ds(pos_ref[0, k, r], 1), :], sem)

    def start(r, carry):
        row_copy(0, r).start()
        row_copy(1, r).start()
        return carry

    def wait(r, carry):
        row_copy(0, r).wait()
        row_copy(1, r).wait()
        return carry

    lax.fori_loop(0, tm, start, 0)
    lax.fori_loop(0, tm, wait, 0)


def _dispatch(h, pos, n_slots, tm=ROW_TILE):
    t, d = h.shape
    return pl.pallas_call(
        _dispatch_kernel,
        grid=(t // tm,),
        in_specs=[pl.BlockSpec((1, 2, tm), lambda i: (i, 0, 0), memory_space=pltpu.SMEM),
                  pl.BlockSpec((tm, d), lambda i: (i, 0)),
                  pl.BlockSpec(memory_space=pl.ANY)],
        out_specs=pl.BlockSpec(memory_space=pl.ANY),
        out_shape=jax.ShapeDtypeStruct((n_slots, d), F32),
        input_output_aliases={2: 0},
        scratch_shapes=[pltpu.SemaphoreType.DMA(())],
        compiler_params=_cparams(("arbitrary",)),
        name="moe_dispatch",
    )(pos, h, jnp.zeros((n_slots, d), F32))


def _moe_kernel(tile_e_ref, n_used_ref, x_ref, w1_ref, w3_ref, w2_ref, y_ref, w1b, w3b, w2b):
    i = pl.program_id(0)
    used = i < n_used_ref[0]
    new_expert = (i == 0) | (tile_e_ref[i] != tile_e_ref[jnp.maximum(i - 1, 0)])

    @pl.when(used & new_expert)
    def _():
        w1b[...] = w1_ref[0, 0].astype(BF16)
        w3b[...] = w3_ref[0, 0].astype(BF16)
        w2b[...] = w2_ref[0, 0].astype(BF16)

    @pl.when(used)
    def _():
        x = x_ref[...].astype(BF16)
        a = jnp.dot(x, w1b[...], preferred_element_type=F32)
        b = jnp.dot(x, w3b[...], preferred_element_type=F32)
        hid = _silu(a) * b
        y_ref[...] = jnp.dot(hid.astype(BF16), w2b[...], preferred_element_type=F32)

    @pl.when(i >= n_used_ref[0])
    def _():
        y_ref[...] = jnp.zeros_like(y_ref)


def _moe_experts(xs, tile_e, n_used, w1, w3, w2, layer, tm=MOE_TILE):
    n_slots, d = xs.shape
    ff = w1.shape[3]

    def row_map(i, te, nu):
        return (jnp.minimum(i, nu[0] - 1), 0)

    def w_map(i, te, nu):
        return (layer, te[i], 0, 0)

    grid_spec = pltpu.PrefetchScalarGridSpec(
        num_scalar_prefetch=2,
        grid=(n_slots // tm,),
        in_specs=[pl.BlockSpec((tm, d), row_map),
                  pl.BlockSpec((1, 1, d, ff), w_map),
                  pl.BlockSpec((1, 1, d, ff), w_map),
                  pl.BlockSpec((1, 1, ff, d), w_map)],
        out_specs=pl.BlockSpec((tm, d), lambda i, te, nu: (i, 0)),
        scratch_shapes=[pltpu.VMEM((d, ff), BF16), pltpu.VMEM((d, ff), BF16), pltpu.VMEM((ff, d), BF16)],
    )
    return pl.pallas_call(
        _moe_kernel,
        grid_spec=grid_spec,
        out_shape=jax.ShapeDtypeStruct((n_slots, d), F32),
        compiler_params=_cparams(("arbitrary",)),
        name="moe_experts",
    )(tile_e, n_used, xs, w1, w3, w2)


def _combine_kernel(final, pos_ref, ys_hbm, wt_ref, x_ref, gate_ref, fg_ref, xo_ref, *rest):
    if final:
        yo_ref, buf, sem = rest
    else:
        buf, sem = rest
    tm = x_ref.shape[0]

    def row_copy(k, r):
        return pltpu.make_async_copy(ys_hbm.at[pl.ds(pos_ref[0, k, r], 1), :], buf.at[k, pl.ds(r, 1), :], sem)

    def start(r, carry):
        row_copy(0, r).start()
        row_copy(1, r).start()
        return carry

    def wait(r, carry):
        row_copy(0, r).wait()
        row_copy(1, r).wait()
        return carry

    lax.fori_loop(0, tm, start, 0)
    lax.fori_loop(0, tm, wait, 0)
    wt = wt_ref[...]
    x = x_ref[...] + gate_ref[0] * (wt[:, 0:1] * buf[0] + wt[:, 1:2] * buf[1])
    xo_ref[...] = x
    if final:
        yo_ref[...] = _rms(x, fg_ref[...])


def _moe_combine(tok, ys, pos, wt_col, x, modr, layer, final_gain, final, tm=ROW_TILE):
    t, d = x.shape
    out_specs = [pl.BlockSpec((tm, d), lambda i: (i, 0))]
    out_shape = [jax.ShapeDtypeStruct((t, d), F32)]
    if final:
        out_specs.append(pl.BlockSpec((tm, d), lambda i: (i, 0)))
        out_shape.append(jax.ShapeDtypeStruct((t, d), F32))
    return pl.pallas_call(
        functools.partial(_combine_kernel, final),
        grid=(t // tm,),
        in_specs=[pl.BlockSpec((1, 2, tm), lambda i: (i, 0, 0), memory_space=pltpu.SMEM),
                  pl.BlockSpec(memory_space=pl.ANY),
                  pl.BlockSpec((tm, 2), lambda i: (i, 0)),
                  pl.BlockSpec((tm, d), lambda i: (i, 0)),
                  _mod_spec(tok, tm, layer, 5, d),
                  pl.BlockSpec((1, d), lambda i: (0, 0))],
        out_specs=out_specs,
        out_shape=out_shape,
        scratch_shapes=[pltpu.VMEM((2, tm, d), F32), pltpu.SemaphoreType.DMA(())],
        compiler_params=_cparams(("arbitrary",)),
        name="moe_combine",
    )(pos, ys, wt_col, x, modr, final_gain.reshape(1, d))


def _slot_tables(eid, rank, counts, tm):
    n_tiles = eid.size // tm + N_EXPERTS
    tiles_per_e = (counts + tm - 1) // tm
    tile_end = jnp.cumsum(tiles_per_e)
    slot0 = (tile_end - tiles_per_e) * tm
    n_used = tile_end[-1:].astype(jnp.int32)
    tile_e = jnp.minimum(jnp.searchsorted(tile_end, jnp.arange(n_tiles), side="right"),
                         N_EXPERTS - 1).astype(jnp.int32)
    experts = jnp.arange(N_EXPERTS, dtype=jnp.int32)
    pos = rank + jnp.sum(jnp.where(eid[..., None] == experts, slot0.astype(jnp.int32), 0), axis=-1)
    return tile_e, n_used, pos, n_tiles * tm


def _moe(tok, h, x, w_rt, b_rt, w1, w3, w2, modr, layer, final_gain, final):
    eid, wt, rank, cnt = _router(h, w_rt, b_rt)
    tile_e, n_used, pos, n_slots = _slot_tables(eid, rank, cnt[:, 0].astype(jnp.int32), MOE_TILE)
    xs = _dispatch(h, pos, n_slots)
    ys = _moe_experts(xs, tile_e, n_used, w1, w3, w2, layer)
    wt_col = wt.transpose(0, 2, 1).reshape(-1, 2)
    return _moe_combine(tok, ys, pos, wt_col, x, modr, layer, final_gain, final)


def _pick_tile(n, candidates):
    for c in candidates:
        if n % c == 0:
            return c
    raise ValueError(f"no tile for {n}")


def kernel(x_prompt, x_sample, cache_attn_k, cache_attn_v, state_ret, state_delta, c, c_ctx, w_mod, b_mod, norm1, norm2, w_in_even, diff_lambda, subln_gain, ret_decay, ret_norm, w_out_even, w_in_odd, conv_w, dn_a_log, dn_dt_bias, dn_norm, w_out_odd, moe_w_group, moe_b_group, moe_w_router, moe_b_router, moe_w1, moe_w3, moe_w2, final_norm):
    n_ctx, l_ctx, d = x_prompt.shape
    n_lat, l_lat, _ = x_sample.shape
    depth = w_mod.shape[0]
    tok = _Tokens(n_ctx, l_ctx, n_lat, l_lat)
    assert 1 + n_lat <= MAX_CONDS and l_ctx % ROW_TILE == 0 and l_lat % ROW_TILE == 0
    h_a = w_in_even.shape[2] // (7 * HEAD)
    h_c = dn_a_log.shape[2]
    d_c = h_c * HEAD

    x = jnp.concatenate([x_prompt.reshape(-1, d), x_sample.reshape(-1, d)], axis=0)
    cond = jnp.zeros((MAX_CONDS, d), F32).at[0].set(c_ctx).at[1:1 + n_lat].set(c)
    mod = _modulation(cond, w_mod, b_mod)
    modr = mod.reshape(depth, MAX_CONDS, 6, d).transpose(0, 2, 1, 3).reshape(depth * 6 * MAX_CONDS, 1, d)

    rt_rows = 4 * SUBLANES
    w_rt = jnp.concatenate([moe_w_group, moe_w_router], axis=2).transpose(0, 2, 1)
    w_rt = jnp.pad(w_rt, ((0, 0), (0, rt_rows - w_rt.shape[1]), (0, 0)))
    b_rt = jnp.pad(jnp.concatenate([moe_b_group, moe_b_router], axis=1),
                   ((0, 0), (0, rt_rows - N_GROUPS - N_EXPERTS)))[:, :, None]

    ks, vs, rets, deltas = [], [], [], []
    y = None
    for layer in range(depth):
        i = layer // 2
        if layer % 2 == 0:
            w_in = w_in_even[i].astype(BF16)
            p = _in_proj(tok, x, norm1[layer], modr, layer, w_in, 512, _pick_tile(w_in.shape[1], (1792, 1024, 512)))
            prm = (diff_lambda[i], subln_gain[i], ret_norm[i], ret_decay[i])
            oa_c, or_c, k_new, v_new, s_new = _even_ctx(tok, p, layer, *prm)
            oa_l, or_l = _even_lat(tok, p, layer, i, *prm, cache_attn_k, cache_attn_v, state_ret)
            ks.append(k_new)
            vs.append(v_new)
            rets.append(s_new)
            outs = [jnp.concatenate([oa_c, oa_l], axis=0), jnp.concatenate([or_c, or_l], axis=0)]
            w_out = w_out_even[i].astype(BF16)
            ws = [w_out[:h_a * HEAD], w_out[h_a * HEAD:]]
        else:
            w_in = w_in_odd[i]
            n_main = 4 * d_c
            p = _in_proj(tok, x, norm1[layer], modr, layer, w_in[:, :n_main].astype(BF16), 512, 1024)
            w_ab = jnp.pad(w_in[:, n_main:], ((0, 0), (0, LANES - 4 * h_c))).astype(BF16)
            ab = _in_proj(tok, x, norm1[layer], modr, layer, w_ab, 512, LANES)
            gc, gr = _delta_gate_layouts(ab, h_c)
            par = jnp.broadcast_to(jnp.concatenate([dn_a_log[i], dn_dt_bias[i]], axis=0).T[:, :, None],
                                   (h_c, 4, LANES))
            o_c, s_new = _delta(tok, p, gc, gr, par, conv_w[i], dn_norm[i], True)
            o_l, = _delta(tok, p, gc, gr, par, conv_w[i], dn_norm[i], False, state_delta, i)
            deltas.append(s_new)
            outs = [jnp.concatenate([o_c, o_l], axis=0)]
            ws = [w_out_odd[i].astype(BF16)]
        x, h2 = _out_proj(tok, outs, ws, x, norm2[layer], modr, layer)
        final = layer == depth - 1
        res = _moe(tok, h2, x, w_rt[layer], b_rt[layer], moe_w1, moe_w3, moe_w2, modr, layer, final_norm, final)
        x = res[0]
        if final:
            y = res[1]

    y_prompt = y[:tok.t_ctx].reshape(n_ctx, l_ctx, d)
    y_sample = y[tok.t_ctx:].reshape(n_lat, l_lat, d)
    return (y_prompt, y_sample, jnp.concatenate(ks, axis=1), jnp.concatenate(vs, axis=1),
            jnp.concatenate(rets, axis=1), jnp.concatenate(deltas, axis=1))
```

```python
import functools
import math

import jax
import jax.numpy as jnp
from jax import lax
from jax.experimental import pallas as pl
from jax.experimental.pallas import tpu as pltpu

F32 = jnp.float32
BF16 = jnp.bfloat16
HI = lax.Precision.HIGHEST

GRID_W = 64
DA = 64
HEAD = 128
CONV_K = 5
CHUNK = 64
ROPE_BASE = 10000.0
N_GROUPS = 4
E_PER_GROUP = 4
N_EXPERTS = N_GROUPS * E_PER_GROUP
EPS = 1e-6

LANES = 128
SUBLANES = 8
MAX_CONDS = SUBLANES
ROW_TILE = 256
MOE_TILE = 256
DMA_UNROLL = 8
VMEM_LIMIT = 56 * 1024 * 1024


def _cparams(sem):
    return pltpu.CompilerParams(dimension_semantics=sem, vmem_limit_bytes=VMEM_LIMIT)


def _bdot(a, b):
    return jnp.dot(a.astype(BF16), b.astype(BF16), preferred_element_type=F32)


def _bdot_nt(a, b):
    return lax.dot_general(a.astype(BF16), b.astype(BF16), (((1,), (1,)), ((), ())),
                           preferred_element_type=F32)


def _bdot_tn(a, b):
    return lax.dot_general(a.astype(BF16), b.astype(BF16), (((0,), (0,)), ((), ())),
                           preferred_element_type=F32)


def _hdot(a, b):
    return jnp.dot(a, b, precision=HI, preferred_element_type=F32)


def _idot(a, b):
    return _bdot(a, b)


def _sigmoid(x):
    return 1.0 / (1.0 + jnp.exp(-x))


def _silu(x):
    return x * _sigmoid(x)


def _softplus(x):
    return jnp.maximum(x, 0.0) + jnp.log(1.0 + jnp.exp(-jnp.abs(x)))


def _rms(x, gain):
    return x * lax.rsqrt(jnp.mean(x * x, axis=-1, keepdims=True) + EPS) * gain


def _softmax(s):
    m = jnp.max(s, axis=-1, keepdims=True)
    e = jnp.exp(s - m)
    return e / jnp.sum(e, axis=-1, keepdims=True)


def _mod_kernel(c_ref, w_ref, b_ref, o_ref):
    c = c_ref[...]
    o_ref[0] = _bdot(_silu(c), w_ref[0]) + b_ref[0]


def _modulation(cond, w_mod, b_mod, tn=1024):
    depth, d, n = w_mod.shape
    return pl.pallas_call(
        _mod_kernel,
        grid=(depth, n // tn),
        in_specs=[pl.BlockSpec((MAX_CONDS, d), lambda l, j: (0, 0)),
                  pl.BlockSpec((1, d, tn), lambda l, j: (l, 0, j)),
                  pl.BlockSpec((1, 1, tn), lambda l, j: (l, 0, j))],
        out_specs=pl.BlockSpec((1, MAX_CONDS, tn), lambda l, j: (l, 0, j)),
        out_shape=jax.ShapeDtypeStruct((depth, MAX_CONDS, n), F32),
        compiler_params=_cparams(("parallel", "parallel")),
        name="adaln_mod",
    )(cond, w_mod, b_mod.reshape(depth, 1, n))


class _Tokens:
    def __init__(self, n_ctx, l_ctx, n_lat, l_lat):
        self.n_ctx, self.l_ctx, self.n_lat, self.l_lat = n_ctx, l_ctx, n_lat, l_lat
        self.t_ctx = n_ctx * l_ctx
        self.t = self.t_ctx + n_lat * l_lat

    def cond_row(self, i, tm):
        n0 = self.t_ctx // tm
        return jnp.where(i < n0, 0, 1 + (i - n0) // (self.l_lat // tm))


def _mod_spec(tok, tm, layer, which, d):
    base = (layer * 6 + which) * MAX_CONDS
    return pl.BlockSpec((1, 1, d), lambda i, *_: (base + tok.cond_row(i, tm), 0, 0))


def _inproj_kernel(n_side, x_ref, g_ref, sh_ref, sc_ref, w_ref, *rest):
    side_w = rest[:2 * n_side]
    o_ref = rest[2 * n_side]
    side_o = rest[2 * n_side + 1:-1]
    h_scr = rest[-1]

    @pl.when(pl.program_id(1) == 0)
    def _():
        h = _rms(x_ref[...], g_ref[...]) * (1.0 + sc_ref[0]) + sh_ref[0]
        hb = h.astype(BF16)
        h_scr[...] = hb
        for s in range(n_side):
            side_o[2 * s][...] = jnp.dot(hb, side_w[2 * s][...], preferred_element_type=F32)
            side_o[2 * s + 1][...] = lax.dot_general(side_w[2 * s + 1][...], hb, (((1,), (1,)), ((), ())),
                                                     preferred_element_type=F32)

    o_ref[...] = jnp.dot(h_scr[...], w_ref[...], preferred_element_type=F32)


def _in_proj(tok, x, gain, modr, layer, w, tm, tn, side_w=None):
    t, d = x.shape
    n = w.shape[1]
    n_side = 0 if side_w is None else 1
    in_specs = [pl.BlockSpec((tm, d), lambda i, j: (i, 0)),
                pl.BlockSpec((1, d), lambda i, j: (0, 0)),
                _mod_spec(tok, tm, layer, 0, d),
                _mod_spec(tok, tm, layer, 1, d),
                pl.BlockSpec((d, tn), lambda i, j: (0, j))]
    args = [x, gain.reshape(1, d), modr, modr, w]
    out_specs = [pl.BlockSpec((tm, tn), lambda i, j: (i, j))]
    out_shape = [jax.ShapeDtypeStruct((t, n), F32)]
    if n_side:
        in_specs += [pl.BlockSpec((d, LANES), lambda i, j: (0, 0)), pl.BlockSpec((LANES, d), lambda i, j: (0, 0))]
        args += [side_w, side_w.T]
        out_specs += [pl.BlockSpec((tm, LANES), lambda i, j: (i, 0)), pl.BlockSpec((LANES, tm), lambda i, j: (0, i))]
        out_shape += [jax.ShapeDtypeStruct((t, LANES), F32), jax.ShapeDtypeStruct((LANES, t), F32)]
    return pl.pallas_call(
        functools.partial(_inproj_kernel, n_side),
        grid=(t // tm, n // tn),
        in_specs=in_specs,
        out_specs=out_specs,
        out_shape=out_shape,
        scratch_shapes=[pltpu.VMEM((tm, d), BF16)],
        compiler_params=_cparams(("parallel", "arbitrary")),
        name="in_proj",
    )(*args)


def _outproj_kernel(n_in, n_ctx_tiles, *refs):
    oc_refs = refs[:n_in]
    ol_refs = refs[n_in:2 * n_in]
    w_refs = refs[2 * n_in:3 * n_in]
    x_ref, gate_ref, g2_ref, sh_ref, sc_ref, xo_ref, h_ref = refs[3 * n_in:]
    is_ctx = pl.program_id(0) < n_ctx_tiles
    acc = None
    for oc, ol, w_r in zip(oc_refs, ol_refs, w_refs):
        part = _bdot(jnp.where(is_ctx, oc[...], ol[...]), w_r[...])
        acc = part if acc is None else acc + part
    x = x_ref[...] + gate_ref[0] * acc
    xo_ref[...] = x
    h = _rms(x, g2_ref[...]) * (1.0 + sc_ref[0]) + sh_ref[0]
    h_ref[...] = h


def _out_proj(tok, outs_ctx, outs_lat, ws, x, gain2, modr, layer, tm=ROW_TILE):
    t, d = x.shape
    n_in = len(ws)
    n0 = tok.t_ctx // tm
    in_specs = ([pl.BlockSpec((tm, o.shape[1]), lambda i: (jnp.minimum(i, n0 - 1), 0)) for o in outs_ctx]
                + [pl.BlockSpec((tm, o.shape[1]), lambda i: (jnp.maximum(i - n0, 0), 0)) for o in outs_lat]
                + [pl.BlockSpec(w.shape, lambda i: (0, 0)) for w in ws]
                + [pl.BlockSpec((tm, d), lambda i: (i, 0)),
                   _mod_spec(tok, tm, layer, 2, d),
                   pl.BlockSpec((1, d), lambda i: (0, 0)),
                   _mod_spec(tok, tm, layer, 3, d),
                   _mod_spec(tok, tm, layer, 4, d)])
    return pl.pallas_call(
        functools.partial(_outproj_kernel, n_in, n0),
        grid=(t // tm,),
        in_specs=in_specs,
        out_specs=[pl.BlockSpec((tm, d), lambda i: (i, 0)),
                   pl.BlockSpec((tm, d), lambda i: (i, 0))],
        out_shape=[jax.ShapeDtypeStruct((t, d), F32), jax.ShapeDtypeStruct((t, d), F32)],
        compiler_params=_cparams(("parallel",)),
        name="out_proj",
    )(*outs_ctx, *outs_lat, *ws, x, modr, gain2.reshape(1, d), modr, modr)


def _lambda(dl_ref, lam_init):
    dl = dl_ref[0]
    s1 = jnp.sum(dl[0:1] * dl[1:2], axis=-1, keepdims=True)
    s2 = jnp.sum(dl[2:3] * dl[3:4], axis=-1, keepdims=True)
    return jnp.exp(s1) - jnp.exp(s2) + lam_init


def _log_gammas(rdec_ref):
    lg = -_softplus(-rdec_ref[0])
    return lg[0:1, 0:1], lg[1:2, 0:1]


def _diff_attn(q, k1, k2, v, lam):
    p1 = _softmax(_bdot_nt(q[:, :DA], k1) * (DA ** -0.5))
    p2 = _softmax(_bdot_nt(q[:, DA:], k2) * (DA ** -0.5))
    return _bdot(p1 - lam * p2, v)


def _ret_mask(row0, nq, nk, lg_f, lg_b):
    qi = lax.broadcasted_iota(jnp.int32, (nq, nk), 0) + row0
    kj = lax.broadcasted_iota(jnp.int32, (nq, nk), 1)
    diff = (qi - kj).astype(F32)
    fwd = jnp.where(diff >= 0, jnp.exp(jnp.maximum(diff, 0.0) * lg_f), 0.0)
    bwd = jnp.where(diff <= 0, jnp.exp(jnp.maximum(-diff, 0.0) * lg_b), 0.0)
    return fwd + bwd


def _even_ctx_kernel(lam_init, n_prev, qa_ref, ka_ref, va_ref, qb_ref, kb_ref, vb_ref, gb_ref,
                     dl_ref, subln_ref, retn_ref, rdec_ref, *rest):
    if n_prev:
        kp_ref, vp_ref, sp_ref = rest[:3]
        rest = rest[3:]
    oa_ref, or_ref, ko_ref, vo_ref, so_ref = rest
    if n_prev:
        ko_ref[0, 0:n_prev] = kp_ref[0]
        vo_ref[0, 0:n_prev] = vp_ref[0]
        so_ref[0, 0:n_prev] = sp_ref[0]
    n = qa_ref.shape[0]
    lam = _lambda(dl_ref, lam_init)
    ka = ka_ref[...]
    va = va_ref[...]
    ko_ref[0, n_prev, 0] = ka
    vo_ref[0, n_prev, 0] = va
    kab = ka.astype(BF16)
    o = _diff_attn(qa_ref[...], kab[:, :DA], kab[:, DA:], va.astype(BF16), lam)
    oa_ref[...] = _rms(o, subln_ref[...]) * (1.0 - lam_init)

    lg_f, lg_b = _log_gammas(rdec_ref)
    kb = kb_ref[...] * (HEAD ** -0.5)
    vb = vb_ref[...].astype(BF16)
    a = _bdot_nt(qb_ref[...], kb) * _ret_mask(0, n, n, lg_f, lg_b)
    o = _bdot(a, vb)
    or_ref[...] = _rms(o, retn_ref[...]) * _silu(gb_ref[...])
    pos = lax.broadcasted_iota(jnp.int32, (n, 1), 0).astype(F32)
    so_ref[0, n_prev, 0, 0] = _bdot_tn(kb * jnp.exp((n - 1.0 - pos) * lg_f), vb)
    so_ref[0, n_prev, 1, 0] = _bdot_tn(kb * jnp.exp(pos * lg_b), vb)


def _head_spec(rows, col0, row0=0):
    return pl.BlockSpec((rows, HEAD), lambda s, h: (row0 + s, col0 + h))


def _even_param_specs():
    return [pl.BlockSpec((1, 4, DA), lambda s, h: (0, 0, 0)),
            pl.BlockSpec((1, HEAD), lambda s, h: (0, 0)),
            pl.BlockSpec((1, HEAD), lambda s, h: (0, 0)),
            pl.BlockSpec((1, 2, LANES), lambda s, h: (h, 0, 0))]


def _even_params(dl, subln, retn, rdec):
    n_heads = rdec.shape[1]
    rdec_b = jnp.broadcast_to(rdec.T[:, :, None], (n_heads, 2, LANES))
    return dl.reshape(1, 4, DA), subln.reshape(1, HEAD), retn.reshape(1, HEAD), rdec_b


def _even_ctx(tok, p, layer, i, dl, subln, retn, rdec, prev):
    n_heads = p.shape[1] // (7 * HEAD)
    lam_init = 0.8 - 0.6 * math.exp(-0.3 * layer)
    l, nseq = tok.l_ctx, tok.n_ctx
    dh = n_heads * HEAD
    args = [p] * 7 + list(_even_params(dl, subln, retn, rdec))
    in_specs = [_head_spec(l, c * n_heads) for c in range(7)] + _even_param_specs()
    if i:
        in_specs += [pl.BlockSpec((1, i, 1, l, HEAD), lambda s, h: (s, 0, h, 0, 0)),
                     pl.BlockSpec((1, i, 1, l, HEAD), lambda s, h: (s, 0, h, 0, 0)),
                     pl.BlockSpec((1, i, 2, 1, HEAD, HEAD), lambda s, h: (s, 0, 0, h, 0, 0))]
        args += list(prev)
    return pl.pallas_call(
        functools.partial(_even_ctx_kernel, lam_init, i),
        grid=(nseq, n_heads),
        in_specs=in_specs,
        out_specs=[pl.BlockSpec((l, HEAD), lambda s, h: (s, h)),
                   pl.BlockSpec((l, HEAD), lambda s, h: (s, h)),
                   pl.BlockSpec((1, i + 1, 1, l, HEAD), lambda s, h: (s, 0, h, 0, 0)),
                   pl.BlockSpec((1, i + 1, 1, l, HEAD), lambda s, h: (s, 0, h, 0, 0)),
                   pl.BlockSpec((1, i + 1, 2, 1, HEAD, HEAD), lambda s, h: (s, 0, 0, h, 0, 0))],
        out_shape=[jax.ShapeDtypeStruct((tok.t_ctx, dh), F32),
                   jax.ShapeDtypeStruct((tok.t_ctx, dh), F32),
                   jax.ShapeDtypeStruct((nseq, i + 1, n_heads, l, HEAD), F32),
                   jax.ShapeDtypeStruct((nseq, i + 1, n_heads, l, HEAD), F32),
                   jax.ShapeDtypeStruct((nseq, i + 1, 2, n_heads, HEAD, HEAD), F32)],
        compiler_params=_cparams(("parallel", "parallel")),
        name="even_mixer_ctx",
    )(*args)


def _rope(x, cos, sin_signed):
    half = DA // 2
    lane = lax.broadcasted_iota(jnp.int32, x.shape, 1)
    swapped = jnp.where((lane % DA) < half,
                        pltpu.roll(x, x.shape[1] - half, axis=1),
                        pltpu.roll(x, half, axis=1))
    return x * cos + swapped * sin_signed


def _even_lat_kernel(lam_init, tq, qa_ref, ka_ref, va_ref, qb_ref, kb_ref, vb_ref, gb_ref,
                     ck_ref, cv_ref, s0_ref, cos_ref, sin_ref,
                     dl_ref, subln_ref, retn_ref, rdec_ref,
                     oa_ref, or_ref, kall, vall):
    n = qa_ref.shape[0]
    past = ck_ref.shape[3]
    lam = _lambda(dl_ref, lam_init)
    cos = cos_ref[...]
    sin = sin_ref[...]
    kall[0:past, :] = ck_ref[0, 0, 0].astype(BF16)
    kall[past:past + n, :] = _rope(ka_ref[...], cos, sin).astype(BF16)
    vall[0:past, :] = cv_ref[0, 0, 0].astype(BF16)
    vall[past:past + n, :] = va_ref[...].astype(BF16)

    lg_f, lg_b = _log_gammas(rdec_ref)
    kb = (kb_ref[...] * (HEAD ** -0.5)).astype(BF16)
    vb = vb_ref[...].astype(BF16)
    s0f = s0_ref[0, 0, 0, 0].astype(BF16)
    s0b = s0_ref[0, 0, 1, 0].astype(BF16)
    for blk in range(n // tq):
        rows = pl.ds(blk * tq, tq)
        q = _rope(qa_ref[rows, :], cos[blk * tq:(blk + 1) * tq], sin[blk * tq:(blk + 1) * tq])
        o = _diff_attn(q, kall[:, :DA], kall[:, DA:], vall[...], lam)
        oa_ref[rows, :] = _rms(o, subln_ref[...]) * (1.0 - lam_init)

        qb = qb_ref[rows, :]
        a = _bdot_nt(qb, kb) * _ret_mask(blk * tq, tq, n, lg_f, lg_b)
        pos = (lax.broadcasted_iota(jnp.int32, (tq, 1), 0) + blk * tq).astype(F32)
        o = (_bdot(a, vb) + _bdot(qb * jnp.exp((pos + 1.0) * lg_f), s0f)
             + _bdot(qb * jnp.exp((n - pos) * lg_b), s0b))
        or_ref[rows, :] = _rms(o, retn_ref[...]) * _silu(gb_ref[rows, :])


def _rope_tables(n_tok):
    n_freq = DA // 4
    t = jnp.arange(n_tok)
    inv_freq = ROPE_BASE ** (-jnp.arange(n_freq, dtype=F32) / n_freq)
    ang = jnp.concatenate([(t // GRID_W).astype(F32)[:, None] * inv_freq,
                           (t % GRID_W).astype(F32)[:, None] * inv_freq], axis=-1)
    cos, sin = jnp.cos(ang), jnp.sin(ang)
    return jnp.tile(cos, (1, 4)), jnp.tile(jnp.concatenate([-sin, sin], axis=-1), (1, 2))


def _even_lat(tok, p, layer, i, dl, subln, retn, rdec, cache_k, cache_v, state_ret, tq=256):
    n_heads = p.shape[1] // (7 * HEAD)
    lam_init = 0.8 - 0.6 * math.exp(-0.3 * layer)
    l, nseq = tok.l_lat, tok.n_lat
    row0 = tok.t_ctx // l
    past = cache_k.shape[3]
    dh = n_heads * HEAD
    cos, sin = _rope_tables(l)
    args = [p] * 7 + [cache_k, cache_v, state_ret, cos, sin] + list(_even_params(dl, subln, retn, rdec))
    return pl.pallas_call(
        functools.partial(_even_lat_kernel, lam_init, tq),
        grid=(nseq, n_heads),
        in_specs=[_head_spec(l, c * n_heads, row0) for c in range(7)]
        + [pl.BlockSpec((1, 1, 1, past, HEAD), lambda s, h: (s, i, h, 0, 0)),
           pl.BlockSpec((1, 1, 1, past, HEAD), lambda s, h: (s, i, h, 0, 0)),
           pl.BlockSpec((1, 1, 2, 1, HEAD, HEAD), lambda s, h: (s, i, 0, h, 0, 0)),
           pl.BlockSpec((l, HEAD), lambda s, h: (0, 0)),
           pl.BlockSpec((l, HEAD), lambda s, h: (0, 0))]
        + _even_param_specs(),
        out_specs=[pl.BlockSpec((l, HEAD), lambda s, h: (s, h)),
                   pl.BlockSpec((l, HEAD), lambda s, h: (s, h))],
        out_shape=[jax.ShapeDtypeStruct((nseq * l, dh), F32), jax.ShapeDtypeStruct((nseq * l, dh), F32)],
        scratch_shapes=[pltpu.VMEM((past + l, HEAD), BF16), pltpu.VMEM((past + l, HEAD), BF16)],
        compiler_params=_cparams(("parallel", "parallel")),
        name="even_mixer_lat",
    )(*args)


def _conv_silu(x, w):
    n = x.shape[0]
    pad = (CONV_K - 1) // 2
    row = lax.broadcasted_iota(jnp.int32, x.shape, 0)
    acc = x * w[pad:pad + 1]
    for j in range(CONV_K):
        d = j - pad
        if d == 0:
            continue
        shifted = pltpu.roll(x, (-d) % n, axis=0)
        valid = (row + d >= 0) & (row + d < n)
        acc = acc + jnp.where(valid, shifted, 0.0) * w[j:j + 1]
    return _silu(acc)


def _l2n(x):
    return x * lax.rsqrt(jnp.sum(x * x, axis=-1, keepdims=True) + EPS)


TRI_BLOCK = 16


def _unit_tri_solves(lms, rhss):
    c = lms[0].shape[0]
    ri = lax.broadcasted_iota(jnp.int32, (c, c), 0)
    ci = lax.broadcasted_iota(jnp.int32, (c, c), 1)
    blk = TRI_BLOCK
    same = (ri // blk) == (ci // blk)
    pws = [jnp.where(same, -lm, 0.0) for lm in lms]
    ns = list(pws)
    span = 1
    while 2 * span < blk:
        pws = [_idot(pw, pw) for pw in pws]
        ns = [n + pw + _idot(n, pw) for n, pw in zip(ns, pws)]
        span *= 2
    while blk < c:
        pair = ((ri // (2 * blk)) == (ci // (2 * blk))) & ((ri // blk) != (ci // blk))
        cs = [jnp.where(pair, lm, 0.0) for lm in lms]
        xs = [cm + _idot(n, cm) for n, cm in zip(ns, cs)]
        ns = [n - (x + _idot(x, n)) for n, x in zip(ns, xs)]
        blk *= 2
    return [rhs + _idot(n, rhs) for n, rhs in zip(ns, rhss)]


DELTA_UNIT = 4


def _delta_kernel(has_s0, n_prev, q_ref, k_ref, v_ref, z_ref, wq_ref, wk_ref, wv_ref,
                  gc_ref, gr_ref, parr_ref, parc_ref, gain_ref, *rest):
    s0_ref = so_ref = None
    if has_s0:
        s0_ref, o_ref = rest[:2]
        rest = rest[2:]
    else:
        if n_prev:
            sp_ref, rest = rest[0], rest[1:]
        o_ref, so_ref = rest[:2]
        rest = rest[2:]
        if n_prev:
            so_ref[0, 0:n_prev] = sp_ref[0]
    qs, ks, vs, gcol, grow, gf_s, gt_s, b_s, w_s, q_s, o_s, e_s, st_s = rest
    n = q_ref.shape[0]
    n_chunks = n // CHUNK
    c = CHUNK
    n_heads = pl.num_programs(1)
    head = pl.program_id(1)

    qs[...] = _l2n(_conv_silu(q_ref[...], wq_ref[...])) * (HEAD ** -0.5)
    ks[...] = _l2n(_conv_silu(k_ref[...], wk_ref[...]))
    vs[...] = _conv_silu(v_ref[...], wv_ref[...])

    lane = lax.broadcasted_iota(jnp.int32, gc_ref.shape, 1)

    @pl.when(head == 0)
    def _():
        xc = gc_ref[...]
        pr = parr_ref[...]
        gf_s[...] = jnp.where(lane < 2 * n_heads, -jnp.exp(pr[0:1]) * _softplus(xc + pr[1:2]), _sigmoid(xc))
        xr = gr_ref[...]
        sub = lax.broadcasted_iota(jnp.int32, xr.shape, 0)
        pc = parc_ref[...]
        gt_s[...] = jnp.where(sub < 2 * n_heads, -jnp.exp(pc[:, 0:1]) * _softplus(xr + pc[:, 1:2]), _sigmoid(xr))

    gfull = gf_s[...]
    cols = [jnp.sum(jnp.where(lane == r * n_heads + head, gfull, 0.0), axis=1, keepdims=True) for r in range(4)]
    gcol[...] = jnp.concatenate(cols, axis=1)
    for r in range(4):
        row = gt_s[pl.ds(r * n_heads + head, 1), :]
        for j in range(n // LANES):
            grow[j, r:r + 1, :] = row[:, j * LANES:(j + 1) * LANES]

    ri = lax.broadcasted_iota(jnp.int32, (c, c), 0)
    ci = lax.broadcasted_iota(jnp.int32, (c, c), 1)
    masks = [(ri >= ci, ri > ci), (ri <= ci, ri < ci)]

    unit = min(DELTA_UNIT, n_chunks)
    cpl = LANES // c
    assert unit % cpl == 0 and n_chunks % unit == 0

    def phase_a(ui, carry):
        probs = []
        for cc in range(unit):
            cidx = ui * unit + cc
            rows = pl.ds(pl.multiple_of(cidx * c, c), c)
            qc, kc, vc = qs[rows, :], ks[rows, :], vs[rows, :]
            kcb = kc.astype(BF16)
            kk = _bdot_nt(kcb, kcb)
            qkr = _bdot_nt(qc, kcb)
            gcl = gcol[rows, :]
            grw = grow[ui * (unit // cpl) + cc // cpl]
            lane0 = (cc % cpl) * c
            for d in range(2):
                incl, strict = masks[d]
                incl_t = masks[1 - d][0]
                g_c, b_c = gcl[:, d:d + 1], gcl[:, 2 + d:3 + d]
                g_r = grw[d:d + 1, lane0:lane0 + c]
                gcc = jnp.sum(jnp.where(incl, g_r, 0.0), axis=1, keepdims=True)
                gcr = jnp.sum(jnp.where(incl_t, g_c, 0.0), axis=0, keepdims=True)
                decay = jnp.where(incl, jnp.exp(jnp.where(incl, gcc - gcr, 0.0)), 0.0)
                egc = jnp.exp(gcc)
                g_last = jnp.sum(g_c, axis=0, keepdims=True)
                probs.append(dict(
                    d=d, cidx=cidx,
                    lm=jnp.where(strict, b_c * kk * decay, 0.0),
                    rhs=jnp.concatenate([vc * b_c, kc * (b_c * egc)], axis=-1),
                    qk=jnp.where(incl, qkr * decay, 0.0).astype(BF16),
                    kexp=(kc * jnp.exp(g_last - gcc)).astype(BF16),
                    qexp=qc * egc,
                    eg=jnp.broadcast_to(jnp.exp(g_last), (1, HEAD))))
        sols = _unit_tri_solves([p["lm"] for p in probs], [p["rhs"] for p in probs])
        sols = [s.astype(BF16) for s in sols]
        kts = [_bdot_tn(p["kexp"], s) for p, s in zip(probs, sols)]
        qos = [_bdot(p["qk"], s) for p, s in zip(probs, sols)]
        for p, kt, qo in zip(probs, kts, qos):
            d, cidx = p["d"], p["cidx"]
            b_s[d, cidx] = kt[:, :HEAD]
            w_s[d, cidx] = kt[:, HEAD:].astype(BF16)
            o_s[d, cidx] = qo[:, :HEAD]
            q_s[d, cidx] = (p["qexp"] - qo[:, HEAD:]).astype(BF16)
            e_s[d, cidx] = p["eg"]
        return carry

    lax.fori_loop(0, n_chunks // unit, phase_a, 0)

    def phase_b(step, carry):
        nxt = []
        for d, s in enumerate(carry):
            cidx = step if d == 0 else n_chunks - 1 - step
            sb = s.astype(BF16)
            st_s[d, cidx] = sb
            nxt.append(e_s[d, cidx] * s + b_s[d, cidx] - jnp.dot(w_s[d, cidx], sb, preferred_element_type=F32))
        return tuple(nxt)

    if has_s0:
        s_init = (s0_ref[0, 0, 0, 0], s0_ref[0, 0, 1, 0])
    else:
        s_init = (jnp.zeros((HEAD, HEAD), F32), jnp.zeros((HEAD, HEAD), F32))
    s_fin = lax.fori_loop(0, n_chunks, phase_b, s_init, unroll=n_chunks <= DELTA_UNIT)
    if so_ref is not None:
        so_ref[0, n_prev, 0, 0] = s_fin[0]
        so_ref[0, n_prev, 1, 0] = s_fin[1]

    gain = gain_ref[...]

    def phase_c(ui, carry):
        cids = [ui * unit + cc for cc in range(unit)]
        prods = [[jnp.dot(q_s[d, cidx], st_s[d, cidx], preferred_element_type=F32) for d in range(2)]
                 for cidx in cids]
        for cidx, (of, ob) in zip(cids, prods):
            rows = pl.ds(pl.multiple_of(cidx * c, c), c)
            o = (of + o_s[0, cidx]) + (ob + o_s[1, cidx])
            o_ref[rows, :] = _rms(o, gain) * _silu(z_ref[rows, :])
        return carry

    lax.fori_loop(0, n_chunks // unit, phase_c, 0)


def _delta(tok, p, ab, abt, par, conv_w, gain, ctx, j, state):
    n_heads = par.shape[-1]
    dc = n_heads * HEAD
    l, nseq = (tok.l_ctx, tok.n_ctx) if ctx else (tok.l_lat, tok.n_lat)
    row0 = 0 if ctx else tok.t_ctx // l
    nc = l // CHUNK
    flat = jnp.pad(jnp.concatenate([par[0].reshape(-1), par[1].reshape(-1)]).reshape(2, 2 * n_heads),
                   ((0, 0), (0, LANES - 2 * n_heads)))
    in_specs = ([_head_spec(l, c * n_heads, row0) for c in range(4)]
                + [pl.BlockSpec((CONV_K, HEAD), lambda s, h, c=c: (0, c * n_heads + h)) for c in range(3)]
                + [pl.BlockSpec((l, LANES), lambda s, h: (row0 + s, 0)),
                   pl.BlockSpec((LANES, l), lambda s, h: (0, row0 + s)),
                   pl.BlockSpec((2, LANES), lambda s, h: (0, 0)),
                   pl.BlockSpec((LANES, 2), lambda s, h: (0, 0)),
                   pl.BlockSpec((1, HEAD), lambda s, h: (0, 0))])
    args = [p, p, p, p, conv_w, conv_w, conv_w, ab, abt, flat, flat.T, gain.reshape(1, HEAD)]
    out_specs = [pl.BlockSpec((l, HEAD), lambda s, h: (s, h))]
    out_shape = [jax.ShapeDtypeStruct((nseq * l, dc), F32)]
    if ctx:
        out_specs.append(pl.BlockSpec((1, j + 1, 2, 1, HEAD, HEAD), lambda s, h: (s, 0, 0, h, 0, 0)))
        out_shape.append(jax.ShapeDtypeStruct((nseq, j + 1, 2, n_heads, HEAD, HEAD), F32))
        if j:
            in_specs.append(pl.BlockSpec((1, j, 2, 1, HEAD, HEAD), lambda s, h: (s, 0, 0, h, 0, 0)))
            args.append(state)
    else:
        in_specs.append(pl.BlockSpec((1, 1, 2, 1, HEAD, HEAD), lambda s, h: (s, j, 0, h, 0, 0)))
        args.append(state)
    return pl.pallas_call(
        functools.partial(_delta_kernel, not ctx, j if ctx else 0),
        grid=(nseq, n_heads),
        in_specs=in_specs,
        out_specs=out_specs,
        out_shape=out_shape,
        scratch_shapes=[pltpu.VMEM((l, HEAD), F32), pltpu.VMEM((l, HEAD), F32), pltpu.VMEM((l, HEAD), F32),
                        pltpu.VMEM((l, 4), F32), pltpu.VMEM((l // LANES, 4, LANES), F32),
                        pltpu.VMEM((l, LANES), F32), pltpu.VMEM((LANES, l), F32),
                        pltpu.VMEM((2, nc, HEAD, HEAD), F32), pltpu.VMEM((2, nc, HEAD, HEAD), BF16),
                        pltpu.VMEM((2, nc, CHUNK, HEAD), BF16), pltpu.VMEM((2, nc, CHUNK, HEAD), F32),
                        pltpu.VMEM((2, nc, 1, HEAD), F32), pltpu.VMEM((2, nc, HEAD, HEAD), BF16)],
        compiler_params=_cparams(("parallel", "arbitrary")),
        name="delta_ctx" if ctx else "delta_lat",
    )(*args)


def _router_kernel(h_ref, w_ref, b_ref, eid_ref, wt_ref, rank_ref, cnt_ref, tri_scr, run_scr):
    i = pl.program_id(0)
    tm = h_ref.shape[0]
    na = 2 * tm

    @pl.when(i == 0)
    def _():
        run_scr[...] = jnp.zeros_like(run_scr)
        a0 = lax.broadcasted_iota(jnp.int32, (na, na), 0)
        a1 = lax.broadcasted_iota(jnp.int32, (na, na), 1)
        tri_scr[...] = (a0 <= a1).astype(BF16)

    h = h_ref[...]
    w = w_ref[...]
    h_hi = h.astype(BF16)
    h_lo = (h - h_hi.astype(F32)).astype(BF16)
    w_hi = w.astype(BF16)
    w_lo = (w - w_hi.astype(F32)).astype(BF16)
    lt = _bdot_nt(w_hi, h_hi) + (_bdot_nt(w_lo, h_hi) + _bdot_nt(w_hi, h_lo)) + b_ref[...]

    def first_argmax(x, valid=None):
        rows = lax.broadcasted_iota(jnp.int32, x.shape, 0)
        if valid is not None:
            x = jnp.where(valid, x, -jnp.inf)
        m = jnp.max(x, axis=0, keepdims=True)
        idx = jnp.min(jnp.where(x == m, rows, x.shape[0]), axis=0, keepdims=True)
        return m, idx

    gl = lt[0:N_GROUPS]
    gmax, gidx = first_argmax(gl)
    g_w = 1.0 / jnp.sum(jnp.exp(gl - gmax), axis=0, keepdims=True)
    el = jnp.zeros((E_PER_GROUP, tm), F32)
    for g in range(N_GROUPS):
        lo = N_GROUPS + g * E_PER_GROUP
        el = jnp.where(gidx == g, lt[lo:lo + E_PER_GROUP], el)
    v1, i1 = first_argmax(el)
    rows = lax.broadcasted_iota(jnp.int32, el.shape, 0)
    v2, i2 = first_argmax(el, rows != i1)
    e21 = jnp.exp(v2 - v1)
    p1 = 1.0 / (1.0 + e21)
    e1 = gidx * E_PER_GROUP + i1
    e2 = gidx * E_PER_GROUP + i2
    eid_ref[...] = jnp.concatenate([e1, e2], axis=0)[None]
    wt_ref[...] = jnp.concatenate([g_w * p1, g_w * (e21 * p1)], axis=0)[None]

    e_all = jnp.concatenate([e1, e2], axis=1)
    onehot = lax.broadcasted_iota(jnp.int32, (N_EXPERTS, na), 0) == e_all
    csum = jnp.dot(onehot.astype(BF16), tri_scr[...], preferred_element_type=F32)
    run = run_scr[:, 0:1]
    rank = jnp.sum(jnp.where(onehot, csum + run, 0.0), axis=0, keepdims=True) - 1.0
    rank = rank.astype(jnp.int32)
    rank_ref[...] = jnp.concatenate([rank[:, :tm], rank[:, tm:]], axis=0)[None]
    run_new = run_scr[...] + csum[:, na - 1:na]
    run_scr[...] = run_new
    cnt_ref[...] = run_new


def _router(h, w_rt, b_rt, tm=ROW_TILE):
    t, d = h.shape
    nt = t // tm
    nr = w_rt.shape[0]
    tile_spec = pl.BlockSpec((1, 2, tm), lambda i: (i, 0, 0))
    return pl.pallas_call(
        _router_kernel,
        grid=(nt,),
        in_specs=[pl.BlockSpec((tm, d), lambda i: (i, 0)),
                  pl.BlockSpec((nr, d), lambda i: (0, 0)),
                  pl.BlockSpec((nr, 1), lambda i: (0, 0))],
        out_specs=[tile_spec, tile_spec, tile_spec,
                   pl.BlockSpec((N_EXPERTS, LANES), lambda i: (0, 0))],
        out_shape=[jax.ShapeDtypeStruct((nt, 2, tm), jnp.int32),
                   jax.ShapeDtypeStruct((nt, 2, tm), F32),
                   jax.ShapeDtypeStruct((nt, 2, tm), jnp.int32),
                   jax.ShapeDtypeStruct((N_EXPERTS, LANES), F32)],
        scratch_shapes=[pltpu.VMEM((2 * tm, 2 * tm), BF16), pltpu.VMEM((N_EXPERTS, LANES), F32)],
        compiler_params=_cparams(("arbitrary",)),
        name="moe_router",
    )(h, w_rt, b_rt)


def _dispatch_kernel(pos_ref, prev_ref, h_hbm, xs_in, xs_out, sem):
    del xs_in
    i = pl.program_id(0)
    tm = pos_ref.shape[2]

    def row_copy(p_ref, tile, k, r):
        return pltpu.make_async_copy(h_hbm.at[pl.ds(tile * tm + r, 1), :],
                                     xs_out.at[pl.ds(p_ref[0, k, r], 1), :], sem.at[tile % 2])

    def start(r, carry):
        row_copy(pos_ref, i, 0, r).start()
        row_copy(pos_ref, i, 1, r).start()
        return carry

    def wait_for(p_ref, tile):
        def wait(r, carry):
            row_copy(p_ref, tile, 0, r).wait()
            row_copy(p_ref, tile, 1, r).wait()
            return carry
        lax.fori_loop(0, tm, wait, 0, unroll=DMA_UNROLL)

    lax.fori_loop(0, tm, start, 0, unroll=DMA_UNROLL)

    @pl.when(i > 0)
    def _():
        wait_for(prev_ref, i - 1)

    @pl.when(i == pl.num_programs(0) - 1)
    def _():
        wait_for(pos_ref, i)


def _dispatch(h, pos, slots, tm=ROW_TILE):
    t, d = h.shape
    return pl.pallas_call(
        _dispatch_kernel,
        grid=(t // tm,),
        in_specs=[pl.BlockSpec((1, 2, tm), lambda i: (i, 0, 0), memory_space=pltpu.SMEM),
                  pl.BlockSpec((1, 2, tm), lambda i: (jnp.maximum(i - 1, 0), 0, 0), memory_space=pltpu.SMEM),
                  pl.BlockSpec(memory_space=pl.ANY),
                  pl.BlockSpec(memory_space=pl.ANY)],
        out_specs=pl.BlockSpec(memory_space=pl.ANY),
        out_shape=jax.ShapeDtypeStruct(slots.shape, F32),
        input_output_aliases={3: 0},
        scratch_shapes=[pltpu.SemaphoreType.DMA((2,))],
        compiler_params=_cparams(("arbitrary",)),
        name="moe_dispatch",
    )(pos, pos, h, slots)


def _moe_kernel(tile_e_ref, n_used_ref, x_ref, w1_ref, w3_ref, w2_ref, y_ref, w1b, w3b, w2b):
    i = pl.program_id(0)
    used = i < n_used_ref[0]
    new_expert = (i == 0) | (tile_e_ref[i] != tile_e_ref[jnp.maximum(i - 1, 0)])

    @pl.when(used & new_expert)
    def _():
        w1b[...] = w1_ref[0, 0].astype(BF16)
        w3b[...] = w3_ref[0, 0].astype(BF16)
        w2b[...] = w2_ref[0, 0].astype(BF16)

    @pl.when(used)
    def _():
        x = x_ref[...].astype(BF16)
        a = jnp.dot(x, w1b[...], preferred_element_type=F32)
        b = jnp.dot(x, w3b[...], preferred_element_type=F32)
        hid = _silu(a) * b
        y_ref[...] = jnp.dot(hid.astype(BF16), w2b[...], preferred_element_type=F32)

    @pl.when(i >= n_used_ref[0])
    def _():
        y_ref[...] = jnp.zeros_like(y_ref)


def _moe_experts(xs, tile_e, n_used, w1, w3, w2, layer, tm=MOE_TILE):
    n_slots, d = xs.shape
    ff = w1.shape[3]

    def row_map(i, te, nu):
        return (jnp.minimum(i, nu[0] - 1), 0)

    def w_map(i, te, nu):
        return (layer, te[i], 0, 0)

    grid_spec = pltpu.PrefetchScalarGridSpec(
        num_scalar_prefetch=2,
        grid=(n_slots // tm,),
        in_specs=[pl.BlockSpec((tm, d), row_map),
                  pl.BlockSpec((1, 1, d, ff), w_map),
                  pl.BlockSpec((1, 1, d, ff), w_map),
                  pl.BlockSpec((1, 1, ff, d), w_map)],
        out_specs=pl.BlockSpec((tm, d), lambda i, te, nu: (i, 0)),
        scratch_shapes=[pltpu.VMEM((d, ff), BF16), pltpu.VMEM((d, ff), BF16), pltpu.VMEM((ff, d), BF16)],
    )
    return pl.pallas_call(
        _moe_kernel,
        grid_spec=grid_spec,
        out_shape=jax.ShapeDtypeStruct((n_slots, d), F32),
        compiler_params=_cparams(("arbitrary",)),
        name="moe_experts",
    )(tile_e, n_used, xs, w1, w3, w2)


def _combine_kernel(final, pos_ref, next_ref, ys_hbm, wt_ref, x_ref, gate_ref, fg_ref, o_ref, buf, sem):
    i = pl.program_id(0)
    tm = x_ref.shape[0]
    slot = i % 2

    def row_copy(p_ref, s, k, r):
        return pltpu.make_async_copy(ys_hbm.at[pl.ds(p_ref[0, k, r], 1), :], buf.at[s, k, pl.ds(r, 1), :], sem.at[s])

    def start_all(p_ref, s):
        def start(r, carry):
            row_copy(p_ref, s, 0, r).start()
            row_copy(p_ref, s, 1, r).start()
            return carry
        lax.fori_loop(0, tm, start, 0, unroll=DMA_UNROLL)

    @pl.when(i == 0)
    def _():
        start_all(pos_ref, 0)

    @pl.when(i + 1 < pl.num_programs(0))
    def _():
        start_all(next_ref, 1 - slot)

    def wait(r, carry):
        row_copy(pos_ref, slot, 0, r).wait()
        row_copy(pos_ref, slot, 1, r).wait()
        return carry

    lax.fori_loop(0, tm, wait, 0, unroll=DMA_UNROLL)
    wt = wt_ref[...]
    x = x_ref[...] + gate_ref[0] * (wt[:, 0:1] * buf[slot, 0] + wt[:, 1:2] * buf[slot, 1])
    o_ref[...] = _rms(x, fg_ref[...]) if final else x


def _moe_combine(tok, ys, pos, wt_col, x, modr, layer, final_gain, final, tile0, n_tiles, tm=ROW_TILE):
    d = x.shape[1]
    base = (layer * 6 + 5) * MAX_CONDS
    return pl.pallas_call(
        functools.partial(_combine_kernel, final),
        grid=(n_tiles,),
        in_specs=[pl.BlockSpec((1, 2, tm), lambda i: (tile0 + i, 0, 0), memory_space=pltpu.SMEM),
                  pl.BlockSpec((1, 2, tm), lambda i: (tile0 + jnp.minimum(i + 1, n_tiles - 1), 0, 0),
                               memory_space=pltpu.SMEM),
                  pl.BlockSpec(memory_space=pl.ANY),
                  pl.BlockSpec((tm, 2), lambda i: (tile0 + i, 0)),
                  pl.BlockSpec((tm, d), lambda i: (tile0 + i, 0)),
                  pl.BlockSpec((1, 1, d), lambda i: (base + tok.cond_row(tile0 + i, tm), 0, 0)),
                  pl.BlockSpec((1, d), lambda i: (0, 0))],
        out_specs=pl.BlockSpec((tm, d), lambda i: (i, 0)),
        out_shape=jax.ShapeDtypeStruct((n_tiles * tm, d), F32),
        scratch_shapes=[pltpu.VMEM((2, 2, tm, d), F32), pltpu.SemaphoreType.DMA((2,))],
        compiler_params=_cparams(("arbitrary",)),
        name="moe_combine",
    )(pos, pos, ys, wt_col, x, modr, final_gain.reshape(1, d))


def _slot_tables(eid, rank, counts, tm):
    n_tiles = eid.size // tm + N_EXPERTS
    tiles_per_e = (counts + tm - 1) // tm
    tile_end = jnp.cumsum(tiles_per_e)
    slot0 = ((tile_end - tiles_per_e) * tm).astype(jnp.int32)
    n_used = tile_end[-1:].astype(jnp.int32)
    tile_e = jnp.sum(jnp.arange(n_tiles)[:, None] >= tile_end[None, :], axis=1)
    tile_e = jnp.minimum(tile_e, N_EXPERTS - 1).astype(jnp.int32)
    experts = jnp.arange(N_EXPERTS, dtype=jnp.int32)
    pos = rank + jnp.sum(jnp.where(eid[..., None] == experts, slot0, 0), axis=-1)
    return tile_e, n_used, pos


def _moe(tok, h, x, slots, w_rt, b_rt, w1, w3, w2, modr, layer, final_gain, final):
    eid, wt, rank, cnt = _router(h, w_rt, b_rt)
    tile_e, n_used, pos = _slot_tables(eid, rank, cnt[:, 0].astype(jnp.int32), MOE_TILE)
    xs = _dispatch(h, pos, slots)
    ys = _moe_experts(xs, tile_e, n_used, w1, w3, w2, layer)
    wt_col = wt.transpose(0, 2, 1).reshape(-1, 2)
    n_ctx_tiles = tok.t_ctx // ROW_TILE
    n_all_tiles = tok.t // ROW_TILE
    comb = functools.partial(_moe_combine, tok, ys, pos, wt_col, x, modr, layer, final_gain, final)
    if final:
        return (comb(0, n_ctx_tiles), comb(n_ctx_tiles, n_all_tiles - n_ctx_tiles)), xs
    return comb(0, n_all_tiles), xs


def _pick_tile(n, candidates):
    for c in candidates:
        if n % c == 0:
            return c
    raise ValueError(f"no tile for {n}")


def kernel(x_prompt, x_sample, cache_attn_k, cache_attn_v, state_ret, state_delta, c, c_ctx, w_mod, b_mod, norm1, norm2, w_in_even, diff_lambda, subln_gain, ret_decay, ret_norm, w_out_even, w_in_odd, conv_w, dn_a_log, dn_dt_bias, dn_norm, w_out_odd, moe_w_group, moe_b_group, moe_w_router, moe_b_router, moe_w1, moe_w3, moe_w2, final_norm):
    n_ctx, l_ctx, d = x_prompt.shape
    n_lat, l_lat, _ = x_sample.shape
    depth = w_mod.shape[0]
    tok = _Tokens(n_ctx, l_ctx, n_lat, l_lat)
    assert 1 + n_lat <= MAX_CONDS and l_ctx % ROW_TILE == 0 and l_lat % ROW_TILE == 0
    h_a = w_in_even.shape[2] // (7 * HEAD)
    h_c = dn_a_log.shape[2]
    d_c = h_c * HEAD

    x = jnp.concatenate([x_prompt.reshape(-1, d), x_sample.reshape(-1, d)], axis=0)
    cond = jnp.zeros((MAX_CONDS, d), F32).at[0].set(c_ctx).at[1:1 + n_lat].set(c)
    mod = _modulation(cond, w_mod, b_mod)
    modr = mod.reshape(depth, MAX_CONDS, 6, d).transpose(0, 2, 1, 3).reshape(depth * 6 * MAX_CONDS, 1, d)

    rt_rows = 4 * SUBLANES
    w_rt = jnp.concatenate([moe_w_group, moe_w_router], axis=2).transpose(0, 2, 1)
    w_rt = jnp.pad(w_rt, ((0, 0), (0, rt_rows - w_rt.shape[1]), (0, 0)))
    b_rt = jnp.pad(jnp.concatenate([moe_b_group, moe_b_router], axis=1),
                   ((0, 0), (0, rt_rows - N_GROUPS - N_EXPERTS)))[:, :, None]

    even_caches = None
    delta_states = None
    slots = jnp.zeros(((2 * tok.t) // MOE_TILE + N_EXPERTS) * MOE_TILE * d, F32).reshape(-1, d)
    for layer in range(depth):
        i = layer // 2
        if layer % 2 == 0:
            w_in = w_in_even[i].astype(BF16)
            p, = _in_proj(tok, x, norm1[layer], modr, layer, w_in, 512, _pick_tile(w_in.shape[1], (1792, 1024, 512)))
            prm = (diff_lambda[i], subln_gain[i], ret_norm[i], ret_decay[i])
            oa, orr, *even_caches = _even_ctx(tok, p, layer, i, *prm, even_caches)
            outs_ctx = [oa, orr]
            outs_lat = _even_lat(tok, p, layer, i, *prm, cache_attn_k, cache_attn_v, state_ret)
            w_out = w_out_even[i].astype(BF16)
            ws = [w_out[:h_a * HEAD], w_out[h_a * HEAD:]]
        else:
            w_in = w_in_odd[i]
            n_main = 4 * d_c
            w_ab = jnp.pad(w_in[:, n_main:], ((0, 0), (0, LANES - 4 * h_c))).astype(BF16)
            p, ab, abt = _in_proj(tok, x, norm1[layer], modr, layer, w_in[:, :n_main].astype(BF16), 512, 1024, w_ab)
            par = jnp.stack([dn_a_log[i], dn_dt_bias[i]])
            dargs = (tok, p, ab, abt, par, conv_w[i], dn_norm[i])
            o, delta_states = _delta(*dargs, True, i, delta_states)
            outs_ctx = [o]
            outs_lat = _delta(*dargs, False, i, state_delta)
            ws = [w_out_odd[i].astype(BF16)]
        x, h2 = _out_proj(tok, outs_ctx, outs_lat, ws, x, norm2[layer], modr, layer)
        final = layer == depth - 1
        x, slots = _moe(tok, h2, x, slots, w_rt[layer], b_rt[layer], moe_w1, moe_w3, moe_w2, modr, layer,
                        final_norm, final)

    y_ctx, y_lat = x
    return (y_ctx.reshape(n_ctx, l_ctx, d), y_lat.reshape(n_lat, l_lat, d), *even_caches, delta_states)
```

```python
import functools
import math

import jax
import jax.numpy as jnp
from jax import lax
from jax.experimental import pallas as pl
from jax.experimental.pallas import tpu as pltpu

F32 = jnp.float32
BF16 = jnp.bfloat16
HI = lax.Precision.HIGHEST

GRID_W = 64
DA = 64
HEAD = 128
CONV_K = 5
CHUNK = 64
ROPE_BASE = 10000.0
N_GROUPS = 4
E_PER_GROUP = 4
N_EXPERTS = N_GROUPS * E_PER_GROUP
EPS = 1e-6

LANES = 128
SUBLANES = 8
MAX_CONDS = SUBLANES
ROW_TILE = 256
MOE_TILE = 256
DMA_UNROLL = 8
VMEM_LIMIT = 56 * 1024 * 1024


def _cparams(sem):
    return pltpu.CompilerParams(dimension_semantics=sem, vmem_limit_bytes=VMEM_LIMIT)


def _bdot(a, b):
    return jnp.dot(a.astype(BF16), b.astype(BF16), preferred_element_type=F32)


def _bdot_nt(a, b):
    return lax.dot_general(a.astype(BF16), b.astype(BF16), (((1,), (1,)), ((), ())),
                           preferred_element_type=F32)


def _bdot_tn(a, b):
    return lax.dot_general(a.astype(BF16), b.astype(BF16), (((0,), (0,)), ((), ())),
                           preferred_element_type=F32)


def _hdot(a, b):
    return jnp.dot(a, b, precision=HI, preferred_element_type=F32)


def _idot(a, b):
    return _bdot(a, b)


def _sigmoid(x):
    return 1.0 / (1.0 + jnp.exp(-x))


def _silu(x):
    return x * _sigmoid(x)


def _softplus(x):
    return jnp.maximum(x, 0.0) + jnp.log(1.0 + jnp.exp(-jnp.abs(x)))


def _rms(x, gain):
    return x * lax.rsqrt(jnp.mean(x * x, axis=-1, keepdims=True) + EPS) * gain


def _softmax(s):
    m = jnp.max(s, axis=-1, keepdims=True)
    e = jnp.exp(s - m)
    return e / jnp.sum(e, axis=-1, keepdims=True)


def _mod_kernel(c_ref, w_ref, b_ref, o_ref):
    c = c_ref[...]
    o_ref[0] = _bdot(_silu(c), w_ref[0]) + b_ref[0]


def _modulation(cond, w_mod, b_mod, tn=1024):
    depth, d, n = w_mod.shape
    return pl.pallas_call(
        _mod_kernel,
        grid=(depth, n // tn),
        in_specs=[pl.BlockSpec((MAX_CONDS, d), lambda l, j: (0, 0)),
                  pl.BlockSpec((1, d, tn), lambda l, j: (l, 0, j)),
                  pl.BlockSpec((1, 1, tn), lambda l, j: (l, 0, j))],
        out_specs=pl.BlockSpec((1, MAX_CONDS, tn), lambda l, j: (l, 0, j)),
        out_shape=jax.ShapeDtypeStruct((depth, MAX_CONDS, n), F32),
        compiler_params=_cparams(("parallel", "parallel")),
        name="adaln_mod",
    )(cond, w_mod, b_mod.reshape(depth, 1, n))


class _Tokens:
    def __init__(self, n_ctx, l_ctx, n_lat, l_lat):
        self.n_ctx, self.l_ctx, self.n_lat, self.l_lat = n_ctx, l_ctx, n_lat, l_lat
        self.t_ctx = n_ctx * l_ctx
        self.t = self.t_ctx + n_lat * l_lat

    def cond_row(self, i, tm):
        n0 = self.t_ctx // tm
        return jnp.where(i < n0, 0, 1 + (i - n0) // (self.l_lat // tm))


def _mod_spec(tok, tm, layer, which, d):
    base = (layer * 6 + which) * MAX_CONDS
    return pl.BlockSpec((1, 1, d), lambda i, *_: (base + tok.cond_row(i, tm), 0, 0))


def _inproj_kernel(n_side, x_ref, g_ref, sh_ref, sc_ref, w_ref, *rest):
    side_w = rest[:2 * n_side]
    o_ref = rest[2 * n_side]
    side_o = rest[2 * n_side + 1:-1]
    h_scr = rest[-1]

    @pl.when(pl.program_id(1) == 0)
    def _():
        h = _rms(x_ref[...], g_ref[...]) * (1.0 + sc_ref[0]) + sh_ref[0]
        hb = h.astype(BF16)
        h_scr[...] = hb
        for s in range(n_side):
            side_o[2 * s][...] = jnp.dot(hb, side_w[2 * s][...], preferred_element_type=F32)
            side_o[2 * s + 1][...] = lax.dot_general(side_w[2 * s + 1][...], hb, (((1,), (1,)), ((), ())),
                                                     preferred_element_type=F32)

    o_ref[...] = jnp.dot(h_scr[...], w_ref[...], preferred_element_type=F32)


def _in_proj(tok, x, gain, modr, layer, w, tm, tn, side_w=None):
    t, d = x.shape
    n = w.shape[1]
    n_side = 0 if side_w is None else 1
    in_specs = [pl.BlockSpec((tm, d), lambda i, j: (i, 0)),
                pl.BlockSpec((1, d), lambda i, j: (0, 0)),
                _mod_spec(tok, tm, layer, 0, d),
                _mod_spec(tok, tm, layer, 1, d),
                pl.BlockSpec((d, tn), lambda i, j: (0, j))]
    args = [x, gain.reshape(1, d), modr, modr, w]
    out_specs = [pl.BlockSpec((tm, tn), lambda i, j: (i, j))]
    out_shape = [jax.ShapeDtypeStruct((t, n), F32)]
    if n_side:
        in_specs += [pl.BlockSpec((d, LANES), lambda i, j: (0, 0)), pl.BlockSpec((LANES, d), lambda i, j: (0, 0))]
        args += [side_w, side_w.T]
        out_specs += [pl.BlockSpec((tm, LANES), lambda i, j: (i, 0)), pl.BlockSpec((LANES, tm), lambda i, j: (0, i))]
        out_shape += [jax.ShapeDtypeStruct((t, LANES), F32), jax.ShapeDtypeStruct((LANES, t), F32)]
    return pl.pallas_call(
        functools.partial(_inproj_kernel, n_side),
        grid=(t // tm, n // tn),
        in_specs=in_specs,
        out_specs=out_specs,
        out_shape=out_shape,
        scratch_shapes=[pltpu.VMEM((tm, d), BF16)],
        compiler_params=_cparams(("parallel", "arbitrary")),
        name="in_proj",
    )(*args)


def _outproj_kernel(n_in, n_ctx_tiles, *refs):
    oc_refs = refs[:n_in]
    ol_refs = refs[n_in:2 * n_in]
    w_refs = refs[2 * n_in:3 * n_in]
    x_ref, gate_ref, g2_ref, sh_ref, sc_ref, xo_ref, h_ref = refs[3 * n_in:]
    is_ctx = pl.program_id(0) < n_ctx_tiles
    acc = None
    for oc, ol, w_r in zip(oc_refs, ol_refs, w_refs):
        part = _bdot(jnp.where(is_ctx, oc[...], ol[...]), w_r[...])
        acc = part if acc is None else acc + part
    x = x_ref[...] + gate_ref[0] * acc
    xo_ref[...] = x
    h = _rms(x, g2_ref[...]) * (1.0 + sc_ref[0]) + sh_ref[0]
    h_ref[...] = h


def _out_proj(tok, outs_ctx, outs_lat, ws, x, gain2, modr, layer, tm=ROW_TILE):
    t, d = x.shape
    n_in = len(ws)
    n0 = tok.t_ctx // tm
    in_specs = ([pl.BlockSpec((tm, o.shape[1]), lambda i: (jnp.minimum(i, n0 - 1), 0)) for o in outs_ctx]
                + [pl.BlockSpec((tm, o.shape[1]), lambda i: (jnp.maximum(i - n0, 0), 0)) for o in outs_lat]
                + [pl.BlockSpec(w.shape, lambda i: (0, 0)) for w in ws]
                + [pl.BlockSpec((tm, d), lambda i: (i, 0)),
                   _mod_spec(tok, tm, layer, 2, d),
                   pl.BlockSpec((1, d), lambda i: (0, 0)),
                   _mod_spec(tok, tm, layer, 3, d),
                   _mod_spec(tok, tm, layer, 4, d)])
    return pl.pallas_call(
        functools.partial(_outproj_kernel, n_in, n0),
        grid=(t // tm,),
        in_specs=in_specs,
        out_specs=[pl.BlockSpec((tm, d), lambda i: (i, 0)),
                   pl.BlockSpec((tm, d), lambda i: (i, 0))],
        out_shape=[jax.ShapeDtypeStruct((t, d), F32), jax.ShapeDtypeStruct((t, d), F32)],
        compiler_params=_cparams(("parallel",)),
        name="out_proj",
    )(*outs_ctx, *outs_lat, *ws, x, modr, gain2.reshape(1, d), modr, modr)


def _lambda(dl_ref, lam_init):
    dl = dl_ref[0]
    s1 = jnp.sum(dl[0:1] * dl[1:2], axis=-1, keepdims=True)
    s2 = jnp.sum(dl[2:3] * dl[3:4], axis=-1, keepdims=True)
    return jnp.exp(s1) - jnp.exp(s2) + lam_init


def _log_gammas(rdec_ref):
    lg = -_softplus(-rdec_ref[0])
    return lg[0:1, 0:1], lg[1:2, 0:1]


def _diff_attn(q, k1, k2, v, lam):
    p1 = _softmax(_bdot_nt(q[:, :DA], k1) * (DA ** -0.5))
    p2 = _softmax(_bdot_nt(q[:, DA:], k2) * (DA ** -0.5))
    return _bdot(p1 - lam * p2, v)


def _ret_mask(row0, nq, nk, lg_f, lg_b):
    qi = lax.broadcasted_iota(jnp.int32, (nq, nk), 0) + row0
    kj = lax.broadcasted_iota(jnp.int32, (nq, nk), 1)
    diff = (qi - kj).astype(F32)
    fwd = jnp.where(diff >= 0, jnp.exp(jnp.maximum(diff, 0.0) * lg_f), 0.0)
    bwd = jnp.where(diff <= 0, jnp.exp(jnp.maximum(-diff, 0.0) * lg_b), 0.0)
    return fwd + bwd


def _even_ctx_kernel(lam_init, n_prev, qa_ref, ka_ref, va_ref, qb_ref, kb_ref, vb_ref, gb_ref,
                     dl_ref, subln_ref, retn_ref, rdec_ref, *rest):
    if n_prev:
        kp_ref, vp_ref, sp_ref = rest[:3]
        rest = rest[3:]
    oa_ref, or_ref, ko_ref, vo_ref, so_ref = rest
    if n_prev:
        ko_ref[0, 0:n_prev] = kp_ref[0]
        vo_ref[0, 0:n_prev] = vp_ref[0]
        so_ref[0, 0:n_prev] = sp_ref[0]
    n = qa_ref.shape[0]
    lam = _lambda(dl_ref, lam_init)
    ka = ka_ref[...]
    va = va_ref[...]
    ko_ref[0, n_prev, 0] = ka
    vo_ref[0, n_prev, 0] = va
    kab = ka.astype(BF16)
    o = _diff_attn(qa_ref[...], kab[:, :DA], kab[:, DA:], va.astype(BF16), lam)
    oa_ref[...] = _rms(o, subln_ref[...]) * (1.0 - lam_init)

    lg_f, lg_b = _log_gammas(rdec_ref)
    kb = kb_ref[...] * (HEAD ** -0.5)
    vb = vb_ref[...].astype(BF16)
    a = _bdot_nt(qb_ref[...], kb) * _ret_mask(0, n, n, lg_f, lg_b)
    o = _bdot(a, vb)
    or_ref[...] = _rms(o, retn_ref[...]) * _silu(gb_ref[...])
    pos = lax.broadcasted_iota(jnp.int32, (n, 1), 0).astype(F32)
    so_ref[0, n_prev, 0, 0] = _bdot_tn(kb * jnp.exp((n - 1.0 - pos) * lg_f), vb)
    so_ref[0, n_prev, 1, 0] = _bdot_tn(kb * jnp.exp(pos * lg_b), vb)


def _head_spec(rows, col0, row0=0):
    return pl.BlockSpec((rows, HEAD), lambda s, h: (row0 + s, col0 + h))


def _even_param_specs():
    return [pl.BlockSpec((1, 4, DA), lambda s, h: (0, 0, 0)),
            pl.BlockSpec((1, HEAD), lambda s, h: (0, 0)),
            pl.BlockSpec((1, HEAD), lambda s, h: (0, 0)),
            pl.BlockSpec((1, 2, LANES), lambda s, h: (h, 0, 0))]


def _even_params(dl, subln, retn, rdec):
    n_heads = rdec.shape[1]
    rdec_b = jnp.broadcast_to(rdec.T[:, :, None], (n_heads, 2, LANES))
    return dl.reshape(1, 4, DA), subln.reshape(1, HEAD), retn.reshape(1, HEAD), rdec_b


def _even_ctx(tok, p, layer, i, dl, subln, retn, rdec, prev):
    n_heads = p.shape[1] // (7 * HEAD)
    lam_init = 0.8 - 0.6 * math.exp(-0.3 * layer)
    l, nseq = tok.l_ctx, tok.n_ctx
    dh = n_heads * HEAD
    args = [p] * 7 + list(_even_params(dl, subln, retn, rdec))
    in_specs = [_head_spec(l, c * n_heads) for c in range(7)] + _even_param_specs()
    if i:
        in_specs += [pl.BlockSpec((1, i, 1, l, HEAD), lambda s, h: (s, 0, h, 0, 0)),
                     pl.BlockSpec((1, i, 1, l, HEAD), lambda s, h: (s, 0, h, 0, 0)),
                     pl.BlockSpec((1, i, 2, 1, HEAD, HEAD), lambda s, h: (s, 0, 0, h, 0, 0))]
        args += list(prev)
    return pl.pallas_call(
        functools.partial(_even_ctx_kernel, lam_init, i),
        grid=(nseq, n_heads),
        in_specs=in_specs,
        out_specs=[pl.BlockSpec((l, HEAD), lambda s, h: (s, h)),
                   pl.BlockSpec((l, HEAD), lambda s, h: (s, h)),
                   pl.BlockSpec((1, i + 1, 1, l, HEAD), lambda s, h: (s, 0, h, 0, 0)),
                   pl.BlockSpec((1, i + 1, 1, l, HEAD), lambda s, h: (s, 0, h, 0, 0)),
                   pl.BlockSpec((1, i + 1, 2, 1, HEAD, HEAD), lambda s, h: (s, 0, 0, h, 0, 0))],
        out_shape=[jax.ShapeDtypeStruct((tok.t_ctx, dh), F32),
                   jax.ShapeDtypeStruct((tok.t_ctx, dh), F32),
                   jax.ShapeDtypeStruct((nseq, i + 1, n_heads, l, HEAD), F32),
                   jax.ShapeDtypeStruct((nseq, i + 1, n_heads, l, HEAD), F32),
                   jax.ShapeDtypeStruct((nseq, i + 1, 2, n_heads, HEAD, HEAD), F32)],
        compiler_params=_cparams(("parallel", "parallel")),
        name="even_mixer_ctx",
    )(*args)


def _rope(x, cos, sin_signed):
    half = DA // 2
    lane = lax.broadcasted_iota(jnp.int32, x.shape, 1)
    swapped = jnp.where((lane % DA) < half,
                        pltpu.roll(x, x.shape[1] - half, axis=1),
                        pltpu.roll(x, half, axis=1))
    return x * cos + swapped * sin_signed


def _even_lat_kernel(lam_init, tq, qa_ref, ka_ref, va_ref, qb_ref, kb_ref, vb_ref, gb_ref,
                     ck_ref, cv_ref, s0_ref, cos_ref, sin_ref,
                     dl_ref, subln_ref, retn_ref, rdec_ref,
                     oa_ref, or_ref, kall, vall):
    n = qa_ref.shape[0]
    past = ck_ref.shape[3]
    lam = _lambda(dl_ref, lam_init)
    cos = cos_ref[...]
    sin = sin_ref[...]
    kall[0:past, :] = ck_ref[0, 0, 0].astype(BF16)
    kall[past:past + n, :] = _rope(ka_ref[...], cos, sin).astype(BF16)
    vall[0:past, :] = cv_ref[0, 0, 0].astype(BF16)
    vall[past:past + n, :] = va_ref[...].astype(BF16)

    lg_f, lg_b = _log_gammas(rdec_ref)
    kb = (kb_ref[...] * (HEAD ** -0.5)).astype(BF16)
    vb = vb_ref[...].astype(BF16)
    s0f = s0_ref[0, 0, 0, 0].astype(BF16)
    s0b = s0_ref[0, 0, 1, 0].astype(BF16)
    for blk in range(n // tq):
        rows = pl.ds(blk * tq, tq)
        q = _rope(qa_ref[rows, :], cos[blk * tq:(blk + 1) * tq], sin[blk * tq:(blk + 1) * tq])
        o = _diff_attn(q, kall[:, :DA], kall[:, DA:], vall[...], lam)
        oa_ref[rows, :] = _rms(o, subln_ref[...]) * (1.0 - lam_init)

        qb = qb_ref[rows, :]
        a = _bdot_nt(qb, kb) * _ret_mask(blk * tq, tq, n, lg_f, lg_b)
        pos = (lax.broadcasted_iota(jnp.int32, (tq, 1), 0) + blk * tq).astype(F32)
        o = (_bdot(a, vb) + _bdot(qb * jnp.exp((pos + 1.0) * lg_f), s0f)
             + _bdot(qb * jnp.exp((n - pos) * lg_b), s0b))
        or_ref[rows, :] = _rms(o, retn_ref[...]) * _silu(gb_ref[rows, :])


def _rope_tables(n_tok):
    n_freq = DA // 4
    t = jnp.arange(n_tok)
    inv_freq = ROPE_BASE ** (-jnp.arange(n_freq, dtype=F32) / n_freq)
    ang = jnp.concatenate([(t // GRID_W).astype(F32)[:, None] * inv_freq,
                           (t % GRID_W).astype(F32)[:, None] * inv_freq], axis=-1)
    cos, sin = jnp.cos(ang), jnp.sin(ang)
    return jnp.tile(cos, (1, 4)), jnp.tile(jnp.concatenate([-sin, sin], axis=-1), (1, 2))


def _even_lat(tok, p, layer, i, dl, subln, retn, rdec, cache_k, cache_v, state_ret, tq=256):
    n_heads = p.shape[1] // (7 * HEAD)
    lam_init = 0.8 - 0.6 * math.exp(-0.3 * layer)
    l, nseq = tok.l_lat, tok.n_lat
    row0 = tok.t_ctx // l
    past = cache_k.shape[3]
    dh = n_heads * HEAD
    cos, sin = _rope_tables(l)
    args = [p] * 7 + [cache_k, cache_v, state_ret, cos, sin] + list(_even_params(dl, subln, retn, rdec))
    return pl.pallas_call(
        functools.partial(_even_lat_kernel, lam_init, tq),
        grid=(nseq, n_heads),
        in_specs=[_head_spec(l, c * n_heads, row0) for c in range(7)]
        + [pl.BlockSpec((1, 1, 1, past, HEAD), lambda s, h: (s, i, h, 0, 0)),
           pl.BlockSpec((1, 1, 1, past, HEAD), lambda s, h: (s, i, h, 0, 0)),
           pl.BlockSpec((1, 1, 2, 1, HEAD, HEAD), lambda s, h: (s, i, 0, h, 0, 0)),
           pl.BlockSpec((l, HEAD), lambda s, h: (0, 0)),
           pl.BlockSpec((l, HEAD), lambda s, h: (0, 0))]
        + _even_param_specs(),
        out_specs=[pl.BlockSpec((l, HEAD), lambda s, h: (s, h)),
                   pl.BlockSpec((l, HEAD), lambda s, h: (s, h))],
        out_shape=[jax.ShapeDtypeStruct((nseq * l, dh), F32), jax.ShapeDtypeStruct((nseq * l, dh), F32)],
        scratch_shapes=[pltpu.VMEM((past + l, HEAD), BF16), pltpu.VMEM((past + l, HEAD), BF16)],
        compiler_params=_cparams(("parallel", "parallel")),
        name="even_mixer_lat",
    )(*args)


def _conv_silu(x, w):
    n = x.shape[0]
    pad = (CONV_K - 1) // 2
    row = lax.broadcasted_iota(jnp.int32, x.shape, 0)
    acc = x * w[pad:pad + 1]
    for j in range(CONV_K):
        d = j - pad
        if d == 0:
            continue
        shifted = pltpu.roll(x, (-d) % n, axis=0)
        valid = (row + d >= 0) & (row + d < n)
        acc = acc + jnp.where(valid, shifted, 0.0) * w[j:j + 1]
    return _silu(acc)


def _l2n(x):
    return x * lax.rsqrt(jnp.sum(x * x, axis=-1, keepdims=True) + EPS)


TRI_BLOCK = 16


def _unit_tri_solves(lms, rhss):
    c = lms[0].shape[0]
    ri = lax.broadcasted_iota(jnp.int32, (c, c), 0)
    ci = lax.broadcasted_iota(jnp.int32, (c, c), 1)
    blk = TRI_BLOCK
    same = (ri // blk) == (ci // blk)
    pws = [jnp.where(same, -lm, 0.0) for lm in lms]
    ns = list(pws)
    span = 1
    while 2 * span < blk:
        pws = [_idot(pw, pw) for pw in pws]
        ns = [n + pw + _idot(n, pw) for n, pw in zip(ns, pws)]
        span *= 2
    while blk < c:
        pair = ((ri // (2 * blk)) == (ci // (2 * blk))) & ((ri // blk) != (ci // blk))
        cs = [jnp.where(pair, lm, 0.0) for lm in lms]
        xs = [cm + _idot(n, cm) for n, cm in zip(ns, cs)]
        ns = [n - (x + _idot(x, n)) for n, x in zip(ns, xs)]
        blk *= 2
    return [rhs + _idot(n, rhs) for n, rhs in zip(ns, rhss)]


DELTA_UNIT = 4


def _delta_kernel(has_s0, n_prev, q_ref, k_ref, v_ref, z_ref, wq_ref, wk_ref, wv_ref,
                  gc_ref, gr_ref, parr_ref, parc_ref, gain_ref, *rest):
    s0_ref = so_ref = None
    if has_s0:
        s0_ref, o_ref = rest[:2]
        rest = rest[2:]
    else:
        if n_prev:
            sp_ref, rest = rest[0], rest[1:]
        o_ref, so_ref = rest[:2]
        rest = rest[2:]
        if n_prev:
            so_ref[0, 0:n_prev] = sp_ref[0]
    qs, ks, vs, gcol, grow, gf_s, gt_s, b_s, w_s, q_s, o_s, e_s, st_s = rest
    n = q_ref.shape[0]
    n_chunks = n // CHUNK
    c = CHUNK
    n_heads = pl.num_programs(1)
    head = pl.program_id(1)

    qs[...] = _l2n(_conv_silu(q_ref[...], wq_ref[...])) * (HEAD ** -0.5)
    ks[...] = _l2n(_conv_silu(k_ref[...], wk_ref[...]))
    vs[...] = _conv_silu(v_ref[...], wv_ref[...])

    lane = lax.broadcasted_iota(jnp.int32, gc_ref.shape, 1)

    @pl.when(head == 0)
    def _():
        xc = gc_ref[...]
        pr = parr_ref[...]
        gf_s[...] = jnp.where(lane < 2 * n_heads, -jnp.exp(pr[0:1]) * _softplus(xc + pr[1:2]), _sigmoid(xc))
        xr = gr_ref[...]
        sub = lax.broadcasted_iota(jnp.int32, xr.shape, 0)
        pc = parc_ref[...]
        gt_s[...] = jnp.where(sub < 2 * n_heads, -jnp.exp(pc[:, 0:1]) * _softplus(xr + pc[:, 1:2]), _sigmoid(xr))

    gfull = gf_s[...]
    cols = [jnp.sum(jnp.where(lane == r * n_heads + head, gfull, 0.0), axis=1, keepdims=True) for r in range(4)]
    gcol[...] = jnp.concatenate(cols, axis=1)
    for r in range(4):
        row = gt_s[pl.ds(r * n_heads + head, 1), :]
        for j in range(n // LANES):
            grow[j, r:r + 1, :] = row[:, j * LANES:(j + 1) * LANES]

    ri = lax.broadcasted_iota(jnp.int32, (c, c), 0)
    ci = lax.broadcasted_iota(jnp.int32, (c, c), 1)
    masks = [(ri >= ci, ri > ci), (ri <= ci, ri < ci)]

    unit = min(DELTA_UNIT, n_chunks)
    cpl = LANES // c
    assert unit % cpl == 0 and n_chunks % unit == 0

    def phase_a(ui, carry):
        probs = []
        for cc in range(unit):
            cidx = ui * unit + cc
            rows = pl.ds(pl.multiple_of(cidx * c, c), c)
            qc, kc, vc = qs[rows, :], ks[rows, :], vs[rows, :]
            kcb = kc.astype(BF16)
            kk = _bdot_nt(kcb, kcb)
            qkr = _bdot_nt(qc, kcb)
            gcl = gcol[rows, :]
            grw = grow[ui * (unit // cpl) + cc // cpl]
            lane0 = (cc % cpl) * c
            for d in range(2):
                incl, strict = masks[d]
                incl_t = masks[1 - d][0]
                g_c, b_c = gcl[:, d:d + 1], gcl[:, 2 + d:3 + d]
                g_r = grw[d:d + 1, lane0:lane0 + c]
                gcc = jnp.sum(jnp.where(incl, g_r, 0.0), axis=1, keepdims=True)
                gcr = jnp.sum(jnp.where(incl_t, g_c, 0.0), axis=0, keepdims=True)
                decay = jnp.where(incl, jnp.exp(jnp.where(incl, gcc - gcr, 0.0)), 0.0)
                egc = jnp.exp(gcc)
                g_last = jnp.sum(g_c, axis=0, keepdims=True)
                probs.append(dict(
                    d=d, cidx=cidx,
                    lm=jnp.where(strict, b_c * kk * decay, 0.0),
                    rhs=jnp.concatenate([vc * b_c, kc * (b_c * egc)], axis=-1),
                    qk=jnp.where(incl, qkr * decay, 0.0).astype(BF16),
                    kexp=(kc * jnp.exp(g_last - gcc)).astype(BF16),
                    qexp=qc * egc,
                    eg=jnp.broadcast_to(jnp.exp(g_last), (1, HEAD))))
        sols = _unit_tri_solves([p["lm"] for p in probs], [p["rhs"] for p in probs])
        sols = [s.astype(BF16) for s in sols]
        kts = [_bdot_tn(p["kexp"], s) for p, s in zip(probs, sols)]
        qos = [_bdot(p["qk"], s) for p, s in zip(probs, sols)]
        for p, kt, qo in zip(probs, kts, qos):
            d, cidx = p["d"], p["cidx"]
            b_s[d, cidx] = kt[:, :HEAD]
            w_s[d, cidx] = kt[:, HEAD:].astype(BF16)
            o_s[d, cidx] = qo[:, :HEAD]
            q_s[d, cidx] = (p["qexp"] - qo[:, HEAD:]).astype(BF16)
            e_s[d, cidx] = p["eg"]
        return carry

    lax.fori_loop(0, n_chunks // unit, phase_a, 0)

    def phase_b(step, carry):
        nxt = []
        for d, s in enumerate(carry):
            cidx = step if d == 0 else n_chunks - 1 - step
            sb = s.astype(BF16)
            st_s[d, cidx] = sb
            nxt.append(e_s[d, cidx] * s + b_s[d, cidx] - jnp.dot(w_s[d, cidx], sb, preferred_element_type=F32))
        return tuple(nxt)

    if has_s0:
        s_init = (s0_ref[0, 0, 0, 0], s0_ref[0, 0, 1, 0])
    else:
        s_init = (jnp.zeros((HEAD, HEAD), F32), jnp.zeros((HEAD, HEAD), F32))
    s_fin = lax.fori_loop(0, n_chunks, phase_b, s_init, unroll=n_chunks <= DELTA_UNIT)
    if so_ref is not None:
        so_ref[0, n_prev, 0, 0] = s_fin[0]
        so_ref[0, n_prev, 1, 0] = s_fin[1]

    gain = gain_ref[...]

    def phase_c(ui, carry):
        cids = [ui * unit + cc for cc in range(unit)]
        prods = [[jnp.dot(q_s[d, cidx], st_s[d, cidx], preferred_element_type=F32) for d in range(2)]
                 for cidx in cids]
        for cidx, (of, ob) in zip(cids, prods):
            rows = pl.ds(pl.multiple_of(cidx * c, c), c)
            o = (of + o_s[0, cidx]) + (ob + o_s[1, cidx])
            o_ref[rows, :] = _rms(o, gain) * _silu(z_ref[rows, :])
        return carry

    lax.fori_loop(0, n_chunks // unit, phase_c, 0)


def _delta(tok, p, ab, abt, par, conv_w, gain, ctx, j, state):
    n_heads = par.shape[-1]
    dc = n_heads * HEAD
    l, nseq = (tok.l_ctx, tok.n_ctx) if ctx else (tok.l_lat, tok.n_lat)
    row0 = 0 if ctx else tok.t_ctx // l
    nc = l // CHUNK
    flat = jnp.pad(jnp.concatenate([par[0].reshape(-1), par[1].reshape(-1)]).reshape(2, 2 * n_heads),
                   ((0, 0), (0, LANES - 2 * n_heads)))
    in_specs = ([_head_spec(l, c * n_heads, row0) for c in range(4)]
                + [pl.BlockSpec((CONV_K, HEAD), lambda s, h, c=c: (0, c * n_heads + h)) for c in range(3)]
                + [pl.BlockSpec((l, LANES), lambda s, h: (row0 + s, 0)),
                   pl.BlockSpec((LANES, l), lambda s, h: (0, row0 + s)),
                   pl.BlockSpec((2, LANES), lambda s, h: (0, 0)),
                   pl.BlockSpec((LANES, 2), lambda s, h: (0, 0)),
                   pl.BlockSpec((1, HEAD), lambda s, h: (0, 0))])
    args = [p, p, p, p, conv_w, conv_w, conv_w, ab, abt, flat, flat.T, gain.reshape(1, HEAD)]
    out_specs = [pl.BlockSpec((l, HEAD), lambda s, h: (s, h))]
    out_shape = [jax.ShapeDtypeStruct((nseq * l, dc), F32)]
    if ctx:
        out_specs.append(pl.BlockSpec((1, j + 1, 2, 1, HEAD, HEAD), lambda s, h: (s, 0, 0, h, 0, 0)))
        out_shape.append(jax.ShapeDtypeStruct((nseq, j + 1, 2, n_heads, HEAD, HEAD), F32))
        if j:
            in_specs.append(pl.BlockSpec((1, j, 2, 1, HEAD, HEAD), lambda s, h: (s, 0, 0, h, 0, 0)))
            args.append(state)
    else:
        in_specs.append(pl.BlockSpec((1, 1, 2, 1, HEAD, HEAD), lambda s, h: (s, j, 0, h, 0, 0)))
        args.append(state)
    return pl.pallas_call(
        functools.partial(_delta_kernel, not ctx, j if ctx else 0),
        grid=(nseq, n_heads),
        in_specs=in_specs,
        out_specs=out_specs,
        out_shape=out_shape,
        scratch_shapes=[pltpu.VMEM((l, HEAD), F32), pltpu.VMEM((l, HEAD), F32), pltpu.VMEM((l, HEAD), F32),
                        pltpu.VMEM((l, 4), F32), pltpu.VMEM((l // LANES, 4, LANES), F32),
                        pltpu.VMEM((l, LANES), F32), pltpu.VMEM((LANES, l), F32),
                        pltpu.VMEM((2, nc, HEAD, HEAD), F32), pltpu.VMEM((2, nc, HEAD, HEAD), BF16),
                        pltpu.VMEM((2, nc, CHUNK, HEAD), BF16), pltpu.VMEM((2, nc, CHUNK, HEAD), F32),
                        pltpu.VMEM((2, nc, 1, HEAD), F32), pltpu.VMEM((2, nc, HEAD, HEAD), BF16)],
        compiler_params=_cparams(("parallel", "arbitrary")),
        name="delta_ctx" if ctx else "delta_lat",
    )(*args)


def _router_kernel(h_ref, w_ref, b_ref, eid_ref, wt_ref, rank_ref, cnt_ref, tri_scr, run_scr):
    i = pl.program_id(0)
    tm = h_ref.shape[0]
    na = 2 * tm

    @pl.when(i == 0)
    def _():
        run_scr[...] = jnp.zeros_like(run_scr)
        a0 = lax.broadcasted_iota(jnp.int32, (na, na), 0)
        a1 = lax.broadcasted_iota(jnp.int32, (na, na), 1)
        tri_scr[...] = (a0 <= a1).astype(BF16)

    h = h_ref[...]
    w = w_ref[...]
    h_hi = h.astype(BF16)
    h_lo = (h - h_hi.astype(F32)).astype(BF16)
    w_hi = w.astype(BF16)
    w_lo = (w - w_hi.astype(F32)).astype(BF16)
    lt = _bdot_nt(w_hi, h_hi) + (_bdot_nt(w_lo, h_hi) + _bdot_nt(w_hi, h_lo)) + b_ref[...]

    def first_argmax(x, valid=None):
        rows = lax.broadcasted_iota(jnp.int32, x.shape, 0)
        if valid is not None:
            x = jnp.where(valid, x, -jnp.inf)
        m = jnp.max(x, axis=0, keepdims=True)
        idx = jnp.min(jnp.where(x == m, rows, x.shape[0]), axis=0, keepdims=True)
        return m, idx

    gl = lt[0:N_GROUPS]
    gmax, gidx = first_argmax(gl)
    g_w = 1.0 / jnp.sum(jnp.exp(gl - gmax), axis=0, keepdims=True)
    el = jnp.zeros((E_PER_GROUP, tm), F32)
    for g in range(N_GROUPS):
        lo = N_GROUPS + g * E_PER_GROUP
        el = jnp.where(gidx == g, lt[lo:lo + E_PER_GROUP], el)
    v1, i1 = first_argmax(el)
    rows = lax.broadcasted_iota(jnp.int32, el.shape, 0)
    v2, i2 = first_argmax(el, rows != i1)
    e21 = jnp.exp(v2 - v1)
    p1 = 1.0 / (1.0 + e21)
    e1 = gidx * E_PER_GROUP + i1
    e2 = gidx * E_PER_GROUP + i2
    eid_ref[...] = jnp.concatenate([e1, e2], axis=0)[None]
    wt_ref[...] = jnp.concatenate([g_w * p1, g_w * (e21 * p1)], axis=0)[None]

    e_all = jnp.concatenate([e1, e2], axis=1)
    onehot = lax.broadcasted_iota(jnp.int32, (N_EXPERTS, na), 0) == e_all
    csum = jnp.dot(onehot.astype(BF16), tri_scr[...], preferred_element_type=F32)
    run = run_scr[:, 0:1]
    rank = jnp.sum(jnp.where(onehot, csum + run, 0.0), axis=0, keepdims=True) - 1.0
    rank = rank.astype(jnp.int32)
    rank_ref[...] = jnp.concatenate([rank[:, :tm], rank[:, tm:]], axis=0)[None]
    run_new = run_scr[...] + csum[:, na - 1:na]
    run_scr[...] = run_new
    cnt_ref[...] = run_new


def _router(h, w_rt, b_rt, tm=ROW_TILE):
    t, d = h.shape
    nt = t // tm
    nr = w_rt.shape[0]
    tile_spec = pl.BlockSpec((1, 2, tm), lambda i: (i, 0, 0))
    return pl.pallas_call(
        _router_kernel,
        grid=(nt,),
        in_specs=[pl.BlockSpec((tm, d), lambda i: (i, 0)),
                  pl.BlockSpec((nr, d), lambda i: (0, 0)),
                  pl.BlockSpec((nr, 1), lambda i: (0, 0))],
        out_specs=[tile_spec, tile_spec, tile_spec,
                   pl.BlockSpec((N_EXPERTS, LANES), lambda i: (0, 0))],
        out_shape=[jax.ShapeDtypeStruct((nt, 2, tm), jnp.int32),
                   jax.ShapeDtypeStruct((nt, 2, tm), F32),
                   jax.ShapeDtypeStruct((nt, 2, tm), jnp.int32),
                   jax.ShapeDtypeStruct((N_EXPERTS, LANES), F32)],
        scratch_shapes=[pltpu.VMEM((2 * tm, 2 * tm), BF16), pltpu.VMEM((N_EXPERTS, LANES), F32)],
        compiler_params=_cparams(("arbitrary",)),
        name="moe_router",
    )(h, w_rt, b_rt)


def _dispatch_kernel(pos_ref, prev_ref, h_hbm, xs_in, xs_out, hbuf, load_sem, row_sem):
    del xs_in
    i = pl.program_id(0)
    n = pl.num_programs(0)
    tm = pos_ref.shape[2]

    def tile_load(tile):
        s = tile % 2
        return pltpu.make_async_copy(h_hbm.at[pl.ds(tile * tm, tm), :], hbuf.at[s], load_sem.at[s])

    def row_copy(p_ref, tile, k, r):
        s = tile % 2
        return pltpu.make_async_copy(hbuf.at[s, pl.ds(r, 1), :], xs_out.at[pl.ds(p_ref[0, k, r], 1), :],
                                     row_sem.at[s])

    def wait_rows(p_ref, tile):
        def wait(r, carry):
            row_copy(p_ref, tile, 0, r).wait()
            row_copy(p_ref, tile, 1, r).wait()
            return carry
        lax.fori_loop(0, tm, wait, 0, unroll=DMA_UNROLL)

    @pl.when(i == 0)
    def _():
        tile_load(0).start()

    tile_load(i).wait()

    def start(r, carry):
        row_copy(pos_ref, i, 0, r).start()
        row_copy(pos_ref, i, 1, r).start()
        return carry

    lax.fori_loop(0, tm, start, 0, unroll=DMA_UNROLL)

    @pl.when(i > 0)
    def _():
        wait_rows(prev_ref, i - 1)

    @pl.when(i + 1 < n)
    def _():
        tile_load(i + 1).start()

    @pl.when(i == n - 1)
    def _():
        wait_rows(pos_ref, i)


def _dispatch(h, pos, slots, tm=ROW_TILE):
    t, d = h.shape
    return pl.pallas_call(
        _dispatch_kernel,
        grid=(t // tm,),
        in_specs=[pl.BlockSpec((1, 2, tm), lambda i: (i, 0, 0), memory_space=pltpu.SMEM),
                  pl.BlockSpec((1, 2, tm), lambda i: (jnp.maximum(i - 1, 0), 0, 0), memory_space=pltpu.SMEM),
                  pl.BlockSpec(memory_space=pl.ANY),
                  pl.BlockSpec(memory_space=pl.ANY)],
        out_specs=pl.BlockSpec(memory_space=pl.ANY),
        out_shape=jax.ShapeDtypeStruct(slots.shape, F32),
        input_output_aliases={3: 0},
        scratch_shapes=[pltpu.VMEM((2, tm, d), F32), pltpu.SemaphoreType.DMA((2,)), pltpu.SemaphoreType.DMA((2,))],
        compiler_params=_cparams(("arbitrary",)),
        name="moe_dispatch",
    )(pos, pos, h, slots)


def _moe_kernel(tile_e_ref, n_used_ref, x_ref, w1_ref, w3_ref, w2_ref, y_ref, w1b, w3b, w2b):
    i = pl.program_id(0)
    used = i < n_used_ref[0]
    new_expert = (i == 0) | (tile_e_ref[i] != tile_e_ref[jnp.maximum(i - 1, 0)])

    @pl.when(used & new_expert)
    def _():
        w1b[...] = w1_ref[0, 0].astype(BF16)
        w3b[...] = w3_ref[0, 0].astype(BF16)
        w2b[...] = w2_ref[0, 0].astype(BF16)

    @pl.when(used)
    def _():
        x = x_ref[...].astype(BF16)
        a = jnp.dot(x, w1b[...], preferred_element_type=F32)
        b = jnp.dot(x, w3b[...], preferred_element_type=F32)
        hid = _silu(a) * b
        y_ref[...] = jnp.dot(hid.astype(BF16), w2b[...], preferred_element_type=F32)

    @pl.when(i >= n_used_ref[0])
    def _():
        y_ref[...] = jnp.zeros_like(y_ref)


def _moe_experts(xs, tile_e, n_used, w1, w3, w2, layer, tm=MOE_TILE):
    n_slots, d = xs.shape
    ff = w1.shape[3]

    def row_map(i, te, nu):
        return (jnp.minimum(i, nu[0] - 1), 0)

    def w_map(i, te, nu):
        return (layer, te[i], 0, 0)

    grid_spec = pltpu.PrefetchScalarGridSpec(
        num_scalar_prefetch=2,
        grid=(n_slots // tm,),
        in_specs=[pl.BlockSpec((tm, d), row_map),
                  pl.BlockSpec((1, 1, d, ff), w_map),
                  pl.BlockSpec((1, 1, d, ff), w_map),
                  pl.BlockSpec((1, 1, ff, d), w_map)],
        out_specs=pl.BlockSpec((tm, d), lambda i, te, nu: (i, 0)),
        scratch_shapes=[pltpu.VMEM((d, ff), BF16), pltpu.VMEM((d, ff), BF16), pltpu.VMEM((ff, d), BF16)],
    )
    return pl.pallas_call(
        _moe_kernel,
        grid_spec=grid_spec,
        out_shape=jax.ShapeDtypeStruct((n_slots, d), F32),
        compiler_params=_cparams(("arbitrary",)),
        name="moe_experts",
    )(tile_e, n_used, xs, w1, w3, w2)


def _combine_kernel(final, pos_ref, next_ref, ys_hbm, wt_ref, x_ref, gate_ref, fg_ref, o_ref, buf, sem):
    i = pl.program_id(0)
    tm = x_ref.shape[0]
    slot = i % 2

    def row_copy(p_ref, s, k, r):
        return pltpu.make_async_copy(ys_hbm.at[pl.ds(p_ref[0, k, r], 1), :], buf.at[s, k, pl.ds(r, 1), :], sem.at[s])

    def start_all(p_ref, s):
        def start(r, carry):
            row_copy(p_ref, s, 0, r).start()
            row_copy(p_ref, s, 1, r).start()
            return carry
        lax.fori_loop(0, tm, start, 0, unroll=DMA_UNROLL)

    @pl.when(i == 0)
    def _():
        start_all(pos_ref, 0)

    @pl.when(i + 1 < pl.num_programs(0))
    def _():
        start_all(next_ref, 1 - slot)

    def wait(r, carry):
        row_copy(pos_ref, slot, 0, r).wait()
        row_copy(pos_ref, slot, 1, r).wait()
        return carry

    lax.fori_loop(0, tm, wait, 0, unroll=DMA_UNROLL)
    wt = wt_ref[...]
    x = x_ref[...] + gate_ref[0] * (wt[:, 0:1] * buf[slot, 0] + wt[:, 1:2] * buf[slot, 1])
    o_ref[...] = _rms(x, fg_ref[...]) if final else x


def _moe_combine(tok, ys, pos, wt_col, x, modr, layer, final_gain, final, tile0, n_tiles, tm=ROW_TILE):
    d = x.shape[1]
    base = (layer * 6 + 5) * MAX_CONDS
    return pl.pallas_call(
        functools.partial(_combine_kernel, final),
        grid=(n_tiles,),
        in_specs=[pl.BlockSpec((1, 2, tm), lambda i: (tile0 + i, 0, 0), memory_space=pltpu.SMEM),
                  pl.BlockSpec((1, 2, tm), lambda i: (tile0 + jnp.minimum(i + 1, n_tiles - 1), 0, 0),
                               memory_space=pltpu.SMEM),
                  pl.BlockSpec(memory_space=pl.ANY),
                  pl.BlockSpec((tm, 2), lambda i: (tile0 + i, 0)),
                  pl.BlockSpec((tm, d), lambda i: (tile0 + i, 0)),
                  pl.BlockSpec((1, 1, d), lambda i: (base + tok.cond_row(tile0 + i, tm), 0, 0)),
                  pl.BlockSpec((1, d), lambda i: (0, 0))],
        out_specs=pl.BlockSpec((tm, d), lambda i: (i, 0)),
        out_shape=jax.ShapeDtypeStruct((n_tiles * tm, d), F32),
        scratch_shapes=[pltpu.VMEM((2, 2, tm, d), F32), pltpu.SemaphoreType.DMA((2,))],
        compiler_params=_cparams(("arbitrary",)),
        name="moe_combine",
    )(pos, pos, ys, wt_col, x, modr, final_gain.reshape(1, d))


def _slot_tables(eid, rank, counts, tm):
    n_tiles = eid.size // tm + N_EXPERTS
    tiles_per_e = (counts + tm - 1) // tm
    tile_end = jnp.cumsum(tiles_per_e)
    slot0 = ((tile_end - tiles_per_e) * tm).astype(jnp.int32)
    n_used = tile_end[-1:].astype(jnp.int32)
    tile_e = jnp.sum(jnp.arange(n_tiles)[:, None] >= tile_end[None, :], axis=1)
    tile_e = jnp.minimum(tile_e, N_EXPERTS - 1).astype(jnp.int32)
    experts = jnp.arange(N_EXPERTS, dtype=jnp.int32)
    pos = rank + jnp.sum(jnp.where(eid[..., None] == experts, slot0, 0), axis=-1)
    return tile_e, n_used, pos


def _moe(tok, h, x, slots, w_rt, b_rt, w1, w3, w2, modr, layer, final_gain, final):
    eid, wt, rank, cnt = _router(h, w_rt, b_rt)
    tile_e, n_used, pos = _slot_tables(eid, rank, cnt[:, 0].astype(jnp.int32), MOE_TILE)
    xs = _dispatch(h, pos, slots)
    ys = _moe_experts(xs, tile_e, n_used, w1, w3, w2, layer)
    wt_col = wt.transpose(0, 2, 1).reshape(-1, 2)
    n_ctx_tiles = tok.t_ctx // ROW_TILE
    n_all_tiles = tok.t // ROW_TILE
    comb = functools.partial(_moe_combine, tok, ys, pos, wt_col, x, modr, layer, final_gain, final)
    if final:
        return (comb(0, n_ctx_tiles), comb(n_ctx_tiles, n_all_tiles - n_ctx_tiles)), xs
    return comb(0, n_all_tiles), xs


def _pick_tile(n, candidates):
    for c in candidates:
        if n % c == 0:
            return c
    raise ValueError(f"no tile for {n}")


def kernel(x_prompt, x_sample, cache_attn_k, cache_attn_v, state_ret, state_delta, c, c_ctx, w_mod, b_mod, norm1, norm2, w_in_even, diff_lambda, subln_gain, ret_decay, ret_norm, w_out_even, w_in_odd, conv_w, dn_a_log, dn_dt_bias, dn_norm, w_out_odd, moe_w_group, moe_b_group, moe_w_router, moe_b_router, moe_w1, moe_w3, moe_w2, final_norm):
    n_ctx, l_ctx, d = x_prompt.shape
    n_lat, l_lat, _ = x_sample.shape
    depth = w_mod.shape[0]
    tok = _Tokens(n_ctx, l_ctx, n_lat, l_lat)
    assert 1 + n_lat <= MAX_CONDS and l_ctx % ROW_TILE == 0 and l_lat % ROW_TILE == 0
    h_a = w_in_even.shape[2] // (7 * HEAD)
    h_c = dn_a_log.shape[2]
    d_c = h_c * HEAD

    x = jnp.concatenate([x_prompt.reshape(-1, d), x_sample.reshape(-1, d)], axis=0)
    cond = jnp.zeros((MAX_CONDS, d), F32).at[0].set(c_ctx).at[1:1 + n_lat].set(c)
    mod = _modulation(cond, w_mod, b_mod)
    modr = mod.reshape(depth, MAX_CONDS, 6, d).transpose(0, 2, 1, 3).reshape(depth * 6 * MAX_CONDS, 1, d)

    rt_rows = 4 * SUBLANES
    w_rt = jnp.concatenate([moe_w_group, moe_w_router], axis=2).transpose(0, 2, 1)
    w_rt = jnp.pad(w_rt, ((0, 0), (0, rt_rows - w_rt.shape[1]), (0, 0)))
    b_rt = jnp.pad(jnp.concatenate([moe_b_group, moe_b_router], axis=1),
                   ((0, 0), (0, rt_rows - N_GROUPS - N_EXPERTS)))[:, :, None]

    tm_in = _pick_tile(math.gcd(tok.t_ctx, l_lat), (1024, 512, ROW_TILE))
    even_caches = None
    delta_states = None
    slots = jnp.zeros(((2 * tok.t) // MOE_TILE + N_EXPERTS) * MOE_TILE * d, F32).reshape(-1, d)
    for layer in range(depth):
        i = layer // 2
        if layer % 2 == 0:
            w_in = w_in_even[i].astype(BF16)
            p, = _in_proj(tok, x, norm1[layer], modr, layer, w_in, tm_in, _pick_tile(w_in.shape[1], (1024, 512)))
            prm = (diff_lambda[i], subln_gain[i], ret_norm[i], ret_decay[i])
            oa, orr, *even_caches = _even_ctx(tok, p, layer, i, *prm, even_caches)
            outs_ctx = [oa, orr]
            outs_lat = _even_lat(tok, p, layer, i, *prm, cache_attn_k, cache_attn_v, state_ret)
            w_out = w_out_even[i].astype(BF16)
            ws = [w_out[:h_a * HEAD], w_out[h_a * HEAD:]]
        else:
            w_in = w_in_odd[i]
            n_main = 4 * d_c
            w_ab = jnp.pad(w_in[:, n_main:], ((0, 0), (0, LANES - 4 * h_c))).astype(BF16)
            p, ab, abt = _in_proj(tok, x, norm1[layer], modr, layer, w_in[:, :n_main].astype(BF16), tm_in,
                                  _pick_tile(n_main, (1024, 512)), w_ab)
            par = jnp.stack([dn_a_log[i], dn_dt_bias[i]])
            dargs = (tok, p, ab, abt, par, conv_w[i], dn_norm[i])
            o, delta_states = _delta(*dargs, True, i, delta_states)
            outs_ctx = [o]
            outs_lat = _delta(*dargs, False, i, state_delta)
            ws = [w_out_odd[i].astype(BF16)]
        x, h2 = _out_proj(tok, outs_ctx, outs_lat, ws, x, norm2[layer], modr, layer)
        final = layer == depth - 1
        x, slots = _moe(tok, h2, x, slots, w_rt[layer], b_rt[layer], moe_w1, moe_w3, moe_w2, modr, layer,
                        final_norm, final)

    y_ctx, y_lat = x
    return (y_ctx.reshape(n_ctx, l_ctx, d), y_lat.reshape(n_lat, l_lat, d), *even_caches, delta_states)
```

```python
import functools
import math

import jax
import jax.numpy as jnp
from jax import lax
from jax.experimental import pallas as pl
from jax.experimental.pallas import tpu as pltpu

F32 = jnp.float32
BF16 = jnp.bfloat16
HI = lax.Precision.HIGHEST

GRID_W = 64
DA = 64
HEAD = 128
CONV_K = 5
CHUNK = 64
ROPE_BASE = 10000.0
N_GROUPS = 4
E_PER_GROUP = 4
N_EXPERTS = N_GROUPS * E_PER_GROUP
EPS = 1e-6

LANES = 128
SUBLANES = 8
MAX_CONDS = SUBLANES
ROW_TILE = 256
MOE_TILE = 256
DMA_UNROLL = 8
VMEM_LIMIT = 56 * 1024 * 1024


def _cparams(sem):
    return pltpu.CompilerParams(dimension_semantics=sem, vmem_limit_bytes=VMEM_LIMIT)


def _bdot(a, b):
    return jnp.dot(a.astype(BF16), b.astype(BF16), preferred_element_type=F32)


def _bdot_nt(a, b):
    return lax.dot_general(a.astype(BF16), b.astype(BF16), (((1,), (1,)), ((), ())),
                           preferred_element_type=F32)


def _bdot_tn(a, b):
    return lax.dot_general(a.astype(BF16), b.astype(BF16), (((0,), (0,)), ((), ())),
                           preferred_element_type=F32)


def _hdot(a, b):
    return jnp.dot(a, b, precision=HI, preferred_element_type=F32)


def _idot(a, b):
    return _bdot(a, b)


def _sigmoid(x):
    return 1.0 / (1.0 + jnp.exp(-x))


def _silu(x):
    return x * _sigmoid(x)


def _softplus(x):
    return jnp.maximum(x, 0.0) + jnp.log(1.0 + jnp.exp(-jnp.abs(x)))


def _rms(x, gain):
    return x * lax.rsqrt(jnp.mean(x * x, axis=-1, keepdims=True) + EPS) * gain


def _softmax(s):
    m = jnp.max(s, axis=-1, keepdims=True)
    e = jnp.exp(s - m)
    return e / jnp.sum(e, axis=-1, keepdims=True)


def _mod_kernel(c_ref, w_ref, b_ref, o_ref):
    c = c_ref[...]
    o_ref[0] = _bdot(_silu(c), w_ref[0]) + b_ref[0]


def _modulation(cond, w_mod, b_mod, tn=1024):
    depth, d, n = w_mod.shape
    return pl.pallas_call(
        _mod_kernel,
        grid=(depth, n // tn),
        in_specs=[pl.BlockSpec((MAX_CONDS, d), lambda l, j: (0, 0)),
                  pl.BlockSpec((1, d, tn), lambda l, j: (l, 0, j)),
                  pl.BlockSpec((1, 1, tn), lambda l, j: (l, 0, j))],
        out_specs=pl.BlockSpec((1, MAX_CONDS, tn), lambda l, j: (l, 0, j)),
        out_shape=jax.ShapeDtypeStruct((depth, MAX_CONDS, n), F32),
        compiler_params=_cparams(("parallel", "parallel")),
        name="adaln_mod",
    )(cond, w_mod, b_mod.reshape(depth, 1, n))


class _Tokens:
    def __init__(self, n_ctx, l_ctx, n_lat, l_lat):
        self.n_ctx, self.l_ctx, self.n_lat, self.l_lat = n_ctx, l_ctx, n_lat, l_lat
        self.t_ctx = n_ctx * l_ctx
        self.t = self.t_ctx + n_lat * l_lat

    def cond_row(self, i, tm):
        n0 = self.t_ctx // tm
        return jnp.where(i < n0, 0, 1 + (i - n0) // (self.l_lat // tm))


def _mod_spec(tok, tm, layer, which, d):
    base = (layer * 6 + which) * MAX_CONDS
    return pl.BlockSpec((1, 1, d), lambda i, *_: (base + tok.cond_row(i, tm), 0, 0))


def _inproj_kernel(n_side, x_ref, g_ref, sh_ref, sc_ref, w_ref, *rest):
    side_w = rest[:2 * n_side]
    o_ref = rest[2 * n_side]
    side_o = rest[2 * n_side + 1:-1]
    h_scr = rest[-1]

    @pl.when(pl.program_id(1) == 0)
    def _():
        h = _rms(x_ref[...], g_ref[...]) * (1.0 + sc_ref[0]) + sh_ref[0]
        hb = h.astype(BF16)
        h_scr[...] = hb
        for s in range(n_side):
            side_o[2 * s][...] = jnp.dot(hb, side_w[2 * s][...], preferred_element_type=F32)
            side_o[2 * s + 1][...] = lax.dot_general(side_w[2 * s + 1][...], hb, (((1,), (1,)), ((), ())),
                                                     preferred_element_type=F32)

    o_ref[...] = jnp.dot(h_scr[...], w_ref[...], preferred_element_type=F32)


def _in_proj(tok, x, gain, modr, layer, w, tm, tn, side_w=None):
    t, d = x.shape
    n = w.shape[1]
    n_side = 0 if side_w is None else 1
    in_specs = [pl.BlockSpec((tm, d), lambda i, j: (i, 0)),
                pl.BlockSpec((1, d), lambda i, j: (0, 0)),
                _mod_spec(tok, tm, layer, 0, d),
                _mod_spec(tok, tm, layer, 1, d),
                pl.BlockSpec((d, tn), lambda i, j: (0, j))]
    args = [x, gain.reshape(1, d), modr, modr, w]
    out_specs = [pl.BlockSpec((tm, tn), lambda i, j: (i, j))]
    out_shape = [jax.ShapeDtypeStruct((t, n), F32)]
    if n_side:
        in_specs += [pl.BlockSpec((d, LANES), lambda i, j: (0, 0)), pl.BlockSpec((LANES, d), lambda i, j: (0, 0))]
        args += [side_w, side_w.T]
        out_specs += [pl.BlockSpec((tm, LANES), lambda i, j: (i, 0)), pl.BlockSpec((LANES, tm), lambda i, j: (0, i))]
        out_shape += [jax.ShapeDtypeStruct((t, LANES), F32), jax.ShapeDtypeStruct((LANES, t), F32)]
    return pl.pallas_call(
        functools.partial(_inproj_kernel, n_side),
        grid=(t // tm, n // tn),
        in_specs=in_specs,
        out_specs=out_specs,
        out_shape=out_shape,
        scratch_shapes=[pltpu.VMEM((tm, d), BF16)],
        compiler_params=_cparams(("parallel", "arbitrary")),
        name="in_proj",
    )(*args)


def _outproj_kernel(n_in, n_ctx_tiles, *refs):
    oc_refs = refs[:n_in]
    ol_refs = refs[n_in:2 * n_in]
    w_refs = refs[2 * n_in:3 * n_in]
    x_ref, gate_ref, g2_ref, sh_ref, sc_ref, xo_ref, h_ref = refs[3 * n_in:]
    is_ctx = pl.program_id(0) < n_ctx_tiles
    acc = None
    for oc, ol, w_r in zip(oc_refs, ol_refs, w_refs):
        part = _bdot(jnp.where(is_ctx, oc[...], ol[...]), w_r[...])
        acc = part if acc is None else acc + part
    x = x_ref[...] + gate_ref[0] * acc
    xo_ref[...] = x
    h = _rms(x, g2_ref[...]) * (1.0 + sc_ref[0]) + sh_ref[0]
    h_ref[...] = h


def _out_proj(tok, outs_ctx, outs_lat, ws, x, gain2, modr, layer, tm=ROW_TILE):
    t, d = x.shape
    n_in = len(ws)
    n0 = tok.t_ctx // tm
    in_specs = ([pl.BlockSpec((tm, o.shape[1]), lambda i: (jnp.minimum(i, n0 - 1), 0)) for o in outs_ctx]
                + [pl.BlockSpec((tm, o.shape[1]), lambda i: (jnp.maximum(i - n0, 0), 0)) for o in outs_lat]
                + [pl.BlockSpec(w.shape, lambda i: (0, 0)) for w in ws]
                + [pl.BlockSpec((tm, d), lambda i: (i, 0)),
                   _mod_spec(tok, tm, layer, 2, d),
                   pl.BlockSpec((1, d), lambda i: (0, 0)),
                   _mod_spec(tok, tm, layer, 3, d),
                   _mod_spec(tok, tm, layer, 4, d)])
    return pl.pallas_call(
        functools.partial(_outproj_kernel, n_in, n0),
        grid=(t // tm,),
        in_specs=in_specs,
        out_specs=[pl.BlockSpec((tm, d), lambda i: (i, 0)),
                   pl.BlockSpec((tm, d), lambda i: (i, 0))],
        out_shape=[jax.ShapeDtypeStruct((t, d), F32), jax.ShapeDtypeStruct((t, d), F32)],
        compiler_params=_cparams(("parallel",)),
        name="out_proj",
    )(*outs_ctx, *outs_lat, *ws, x, modr, gain2.reshape(1, d), modr, modr)


def _lambda(dl_ref, lam_init):
    dl = dl_ref[0]
    s1 = jnp.sum(dl[0:1] * dl[1:2], axis=-1, keepdims=True)
    s2 = jnp.sum(dl[2:3] * dl[3:4], axis=-1, keepdims=True)
    return jnp.exp(s1) - jnp.exp(s2) + lam_init


def _log_gammas(rdec_ref):
    lg = -_softplus(-rdec_ref[0])
    return lg[0:1, 0:1], lg[1:2, 0:1]


def _diff_attn(q, k1, k2, v, lam):
    p1 = _softmax(_bdot_nt(q[:, :DA], k1) * (DA ** -0.5))
    p2 = _softmax(_bdot_nt(q[:, DA:], k2) * (DA ** -0.5))
    return _bdot(p1 - lam * p2, v)


def _ret_mask(row0, nq, nk, lg_f, lg_b):
    qi = lax.broadcasted_iota(jnp.int32, (nq, nk), 0) + row0
    kj = lax.broadcasted_iota(jnp.int32, (nq, nk), 1)
    diff = (qi - kj).astype(F32)
    fwd = jnp.where(diff >= 0, jnp.exp(jnp.maximum(diff, 0.0) * lg_f), 0.0)
    bwd = jnp.where(diff <= 0, jnp.exp(jnp.maximum(-diff, 0.0) * lg_b), 0.0)
    return fwd + bwd


def _even_ctx_kernel(lam_init, n_prev, qa_ref, ka_ref, va_ref, qb_ref, kb_ref, vb_ref, gb_ref,
                     dl_ref, subln_ref, retn_ref, rdec_ref, *rest):
    if n_prev:
        kp_ref, vp_ref, sp_ref = rest[:3]
        rest = rest[3:]
    oa_ref, or_ref, ko_ref, vo_ref, so_ref = rest
    if n_prev:
        ko_ref[0, 0:n_prev] = kp_ref[0]
        vo_ref[0, 0:n_prev] = vp_ref[0]
        so_ref[0, 0:n_prev] = sp_ref[0]
    n = qa_ref.shape[0]
    heads = range(qa_ref.shape[1] // HEAD)
    lam = _lambda(dl_ref, lam_init)

    def cols(ref, hh):
        return ref[:, hh * HEAD:(hh + 1) * HEAD]

    ka = [cols(ka_ref, hh) for hh in heads]
    va = [cols(va_ref, hh) for hh in heads]
    for hh in heads:
        ko_ref[0, n_prev, hh] = ka[hh]
        vo_ref[0, n_prev, hh] = va[hh]
    kab = [x.astype(BF16) for x in ka]
    qa = [cols(qa_ref, hh) for hh in heads]
    s1 = [_bdot_nt(qa[hh][:, :DA], kab[hh][:, :DA]) * (DA ** -0.5) for hh in heads]
    s2 = [_bdot_nt(qa[hh][:, DA:], kab[hh][:, DA:]) * (DA ** -0.5) for hh in heads]

    kb = [cols(kb_ref, hh) * (HEAD ** -0.5) for hh in heads]
    vb = [cols(vb_ref, hh).astype(BF16) for hh in heads]
    sr = [_bdot_nt(cols(qb_ref, hh), kb[hh]) for hh in heads]

    att = [_softmax(s1[hh]) - lam * _softmax(s2[hh]) for hh in heads]
    oa = [_bdot(att[hh], va[hh].astype(BF16)) for hh in heads]
    lgs = [_log_gammas(rdec_ref.at[pl.ds(hh, 1)]) for hh in heads]
    ret = [sr[hh] * _ret_mask(0, n, n, *lgs[hh]) for hh in heads]
    orr = [_bdot(ret[hh], vb[hh]) for hh in heads]
    pos = lax.broadcasted_iota(jnp.int32, (n, 1), 0).astype(F32)
    sf = [_bdot_tn(kb[hh] * jnp.exp((n - 1.0 - pos) * lgs[hh][0]), vb[hh]) for hh in heads]
    sb = [_bdot_tn(kb[hh] * jnp.exp(pos * lgs[hh][1]), vb[hh]) for hh in heads]
    for hh in heads:
        lanes = slice(hh * HEAD, (hh + 1) * HEAD)
        oa_ref[:, lanes] = _rms(oa[hh], subln_ref[...]) * (1.0 - lam_init)
        or_ref[:, lanes] = _rms(orr[hh], retn_ref[...]) * _silu(cols(gb_ref, hh))
        so_ref[0, n_prev, 0, hh] = sf[hh]
        so_ref[0, n_prev, 1, hh] = sb[hh]


EVEN_HEADS = 4


def _head_spec(rows, col0, row0=0):
    return pl.BlockSpec((rows, HEAD), lambda s, h: (row0 + s, col0 + h))


def _even_param_specs():
    return [pl.BlockSpec((1, 4, DA), lambda s, h: (0, 0, 0)),
            pl.BlockSpec((1, HEAD), lambda s, h: (0, 0)),
            pl.BlockSpec((1, HEAD), lambda s, h: (0, 0)),
            pl.BlockSpec((1, 2, LANES), lambda s, h: (h, 0, 0))]


def _even_params(dl, subln, retn, rdec):
    n_heads = rdec.shape[1]
    rdec_b = jnp.broadcast_to(rdec.T[:, :, None], (n_heads, 2, LANES))
    return dl.reshape(1, 4, DA), subln.reshape(1, HEAD), retn.reshape(1, HEAD), rdec_b


def _even_ctx(tok, p, layer, i, dl, subln, retn, rdec, prev):
    n_heads = p.shape[1] // (7 * HEAD)
    lam_init = 0.8 - 0.6 * math.exp(-0.3 * layer)
    l, nseq = tok.l_ctx, tok.n_ctx
    dh = n_heads * HEAD
    hb = math.gcd(EVEN_HEADS, n_heads)
    groups = n_heads // hb
    args = [p] * 7 + list(_even_params(dl, subln, retn, rdec))
    in_specs = ([pl.BlockSpec((l, hb * HEAD), lambda s, g, c=c: (s, c * groups + g)) for c in range(7)]
                + _even_param_specs()[:3] + [pl.BlockSpec((hb, 2, LANES), lambda s, g: (g, 0, 0))])
    if i:
        in_specs += [pl.BlockSpec((1, i, hb, l, HEAD), lambda s, g: (s, 0, g, 0, 0)),
                     pl.BlockSpec((1, i, hb, l, HEAD), lambda s, g: (s, 0, g, 0, 0)),
                     pl.BlockSpec((1, i, 2, hb, HEAD, HEAD), lambda s, g: (s, 0, 0, g, 0, 0))]
        args += list(prev)
    return pl.pallas_call(
        functools.partial(_even_ctx_kernel, lam_init, i),
        grid=(nseq, groups),
        in_specs=in_specs,
        out_specs=[pl.BlockSpec((l, hb * HEAD), lambda s, g: (s, g)),
                   pl.BlockSpec((l, hb * HEAD), lambda s, g: (s, g)),
                   pl.BlockSpec((1, i + 1, hb, l, HEAD), lambda s, g: (s, 0, g, 0, 0)),
                   pl.BlockSpec((1, i + 1, hb, l, HEAD), lambda s, g: (s, 0, g, 0, 0)),
                   pl.BlockSpec((1, i + 1, 2, hb, HEAD, HEAD), lambda s, g: (s, 0, 0, g, 0, 0))],
        out_shape=[jax.ShapeDtypeStruct((tok.t_ctx, dh), F32),
                   jax.ShapeDtypeStruct((tok.t_ctx, dh), F32),
                   jax.ShapeDtypeStruct((nseq, i + 1, n_heads, l, HEAD), F32),
                   jax.ShapeDtypeStruct((nseq, i + 1, n_heads, l, HEAD), F32),
                   jax.ShapeDtypeStruct((nseq, i + 1, 2, n_heads, HEAD, HEAD), F32)],
        compiler_params=_cparams(("parallel", "parallel")),
        name="even_mixer_ctx",
    )(*args)


def _rope(x, cos, sin_signed):
    half = DA // 2
    lane = lax.broadcasted_iota(jnp.int32, x.shape, 1)
    swapped = jnp.where((lane % DA) < half,
                        pltpu.roll(x, x.shape[1] - half, axis=1),
                        pltpu.roll(x, half, axis=1))
    return x * cos + swapped * sin_signed


def _even_lat_kernel(lam_init, tq, qa_ref, ka_ref, va_ref, qb_ref, kb_ref, vb_ref, gb_ref,
                     ck_ref, cv_ref, s0_ref, cos_ref, sin_ref,
                     dl_ref, subln_ref, retn_ref, rdec_ref,
                     oa_ref, or_ref, kall, vall):
    n = qa_ref.shape[0]
    past = ck_ref.shape[3]
    lam = _lambda(dl_ref, lam_init)
    cos = cos_ref[...]
    sin = sin_ref[...]
    kall[0:past, :] = ck_ref[0, 0, 0].astype(BF16)
    kall[past:past + n, :] = _rope(ka_ref[...], cos, sin).astype(BF16)
    vall[0:past, :] = cv_ref[0, 0, 0].astype(BF16)
    vall[past:past + n, :] = va_ref[...].astype(BF16)

    lg_f, lg_b = _log_gammas(rdec_ref)
    kb = (kb_ref[...] * (HEAD ** -0.5)).astype(BF16)
    vb = vb_ref[...].astype(BF16)
    s0f = s0_ref[0, 0, 0, 0].astype(BF16)
    s0b = s0_ref[0, 0, 1, 0].astype(BF16)
    for blk in range(n // tq):
        rows = pl.ds(blk * tq, tq)
        q = _rope(qa_ref[rows, :], cos[blk * tq:(blk + 1) * tq], sin[blk * tq:(blk + 1) * tq])
        o = _diff_attn(q, kall[:, :DA], kall[:, DA:], vall[...], lam)
        oa_ref[rows, :] = _rms(o, subln_ref[...]) * (1.0 - lam_init)

        qb = qb_ref[rows, :]
        a = _bdot_nt(qb, kb) * _ret_mask(blk * tq, tq, n, lg_f, lg_b)
        pos = (lax.broadcasted_iota(jnp.int32, (tq, 1), 0) + blk * tq).astype(F32)
        o = (_bdot(a, vb) + _bdot(qb * jnp.exp((pos + 1.0) * lg_f), s0f)
             + _bdot(qb * jnp.exp((n - pos) * lg_b), s0b))
        or_ref[rows, :] = _rms(o, retn_ref[...]) * _silu(gb_ref[rows, :])


def _rope_tables(n_tok):
    n_freq = DA // 4
    t = jnp.arange(n_tok)
    inv_freq = ROPE_BASE ** (-jnp.arange(n_freq, dtype=F32) / n_freq)
    ang = jnp.concatenate([(t // GRID_W).astype(F32)[:, None] * inv_freq,
                           (t % GRID_W).astype(F32)[:, None] * inv_freq], axis=-1)
    cos, sin = jnp.cos(ang), jnp.sin(ang)
    return jnp.tile(cos, (1, 4)), jnp.tile(jnp.concatenate([-sin, sin], axis=-1), (1, 2))


def _even_lat(tok, p, layer, i, dl, subln, retn, rdec, cache_k, cache_v, state_ret, tq=256):
    n_heads = p.shape[1] // (7 * HEAD)
    lam_init = 0.8 - 0.6 * math.exp(-0.3 * layer)
    l, nseq = tok.l_lat, tok.n_lat
    row0 = tok.t_ctx // l
    past = cache_k.shape[3]
    dh = n_heads * HEAD
    cos, sin = _rope_tables(l)
    args = [p] * 7 + [cache_k, cache_v, state_ret, cos, sin] + list(_even_params(dl, subln, retn, rdec))
    return pl.pallas_call(
        functools.partial(_even_lat_kernel, lam_init, tq),
        grid=(nseq, n_heads),
        in_specs=[_head_spec(l, c * n_heads, row0) for c in range(7)]
        + [pl.BlockSpec((1, 1, 1, past, HEAD), lambda s, h: (s, i, h, 0, 0)),
           pl.BlockSpec((1, 1, 1, past, HEAD), lambda s, h: (s, i, h, 0, 0)),
           pl.BlockSpec((1, 1, 2, 1, HEAD, HEAD), lambda s, h: (s, i, 0, h, 0, 0)),
           pl.BlockSpec((l, HEAD), lambda s, h: (0, 0)),
           pl.BlockSpec((l, HEAD), lambda s, h: (0, 0))]
        + _even_param_specs(),
        out_specs=[pl.BlockSpec((l, HEAD), lambda s, h: (s, h)),
                   pl.BlockSpec((l, HEAD), lambda s, h: (s, h))],
        out_shape=[jax.ShapeDtypeStruct((nseq * l, dh), F32), jax.ShapeDtypeStruct((nseq * l, dh), F32)],
        scratch_shapes=[pltpu.VMEM((past + l, HEAD), BF16), pltpu.VMEM((past + l, HEAD), BF16)],
        compiler_params=_cparams(("parallel", "parallel")),
        name="even_mixer_lat",
    )(*args)


def _conv_silu(x, w):
    n = x.shape[0]
    pad = (CONV_K - 1) // 2
    row = lax.broadcasted_iota(jnp.int32, x.shape, 0)
    acc = x * w[pad:pad + 1]
    for j in range(CONV_K):
        d = j - pad
        if d == 0:
            continue
        shifted = pltpu.roll(x, (-d) % n, axis=0)
        valid = (row + d >= 0) & (row + d < n)
        acc = acc + jnp.where(valid, shifted, 0.0) * w[j:j + 1]
    return _silu(acc)


def _l2n(x):
    return x * lax.rsqrt(jnp.sum(x * x, axis=-1, keepdims=True) + EPS)


TRI_BLOCK = 16


def _unit_tri_solves(lms, rhss):
    c = lms[0].shape[0]
    ri = lax.broadcasted_iota(jnp.int32, (c, c), 0)
    ci = lax.broadcasted_iota(jnp.int32, (c, c), 1)
    blk = TRI_BLOCK
    same = (ri // blk) == (ci // blk)
    pws = [jnp.where(same, -lm, 0.0) for lm in lms]
    ns = list(pws)
    span = 1
    while 2 * span < blk:
        pws = [_idot(pw, pw) for pw in pws]
        ns = [n + pw + _idot(n, pw) for n, pw in zip(ns, pws)]
        span *= 2
    while blk < c:
        pair = ((ri // (2 * blk)) == (ci // (2 * blk))) & ((ri // blk) != (ci // blk))
        cs = [jnp.where(pair, lm, 0.0) for lm in lms]
        xs = [cm + _idot(n, cm) for n, cm in zip(ns, cs)]
        ns = [n - (x + _idot(x, n)) for n, x in zip(ns, xs)]
        blk *= 2
    return [rhs + _idot(n, rhs) for n, rhs in zip(ns, rhss)]


DELTA_UNIT = 4
DELTA_HEADS = 2


def _delta_kernel(has_s0, n_prev, n_heads, q_ref, k_ref, v_ref, z_ref, wq_ref, wk_ref, wv_ref,
                  gc_ref, gr_ref, parr_ref, parc_ref, gain_ref, *rest):
    s0_ref = so_ref = None
    if has_s0:
        s0_ref, o_ref = rest[:2]
        rest = rest[2:]
    else:
        if n_prev:
            sp_ref, rest = rest[0], rest[1:]
        o_ref, so_ref = rest[:2]
        rest = rest[2:]
        if n_prev:
            so_ref[0, 0:n_prev] = sp_ref[0]
    qs, ks, vs, gcol, grow, gf_s, gt_s, b_s, w_s, q_s, o_s, e_s, st_s = rest
    n = q_ref.shape[0]
    n_chunks = n // CHUNK
    c = CHUNK
    grp = pl.program_id(1)
    heads = range(q_ref.shape[1] // HEAD)

    def cols(ref, hh):
        return ref[:, hh * HEAD:(hh + 1) * HEAD]

    for hh in heads:
        qs[hh] = _l2n(_conv_silu(cols(q_ref, hh), cols(wq_ref, hh))) * (HEAD ** -0.5)
        ks[hh] = _l2n(_conv_silu(cols(k_ref, hh), cols(wk_ref, hh)))
        vs[hh] = _conv_silu(cols(v_ref, hh), cols(wv_ref, hh))

    lane = lax.broadcasted_iota(jnp.int32, gc_ref.shape, 1)

    @pl.when(grp == 0)
    def _():
        xc = gc_ref[...]
        pr = parr_ref[...]
        gf_s[...] = jnp.where(lane < 2 * n_heads, -jnp.exp(pr[0:1]) * _softplus(xc + pr[1:2]), _sigmoid(xc))
        xr = gr_ref[...]
        sub = lax.broadcasted_iota(jnp.int32, xr.shape, 0)
        pc = parc_ref[...]
        gt_s[...] = jnp.where(sub < 2 * n_heads, -jnp.exp(pc[:, 0:1]) * _softplus(xr + pc[:, 1:2]), _sigmoid(xr))

    gfull = gf_s[...]
    for hh in heads:
        head = grp * len(heads) + hh
        chans = [jnp.sum(jnp.where(lane == r * n_heads + head, gfull, 0.0), axis=1, keepdims=True) for r in range(4)]
        gcol[hh] = jnp.concatenate(chans, axis=1)
        for r in range(4):
            row = gt_s[pl.ds(r * n_heads + head, 1), :]
            for j in range(n // LANES):
                grow[hh, j, r:r + 1, :] = row[:, j * LANES:(j + 1) * LANES]

    ri = lax.broadcasted_iota(jnp.int32, (c, c), 0)
    ci = lax.broadcasted_iota(jnp.int32, (c, c), 1)
    masks = [(ri >= ci, ri > ci), (ri <= ci, ri < ci)]

    unit = min(DELTA_UNIT, n_chunks)
    cpl = LANES // c
    assert unit % cpl == 0 and n_chunks % unit == 0

    def phase_a(hh, ui, carry):
        probs = []
        for cc in range(unit):
            cidx = ui * unit + cc
            rows = pl.ds(pl.multiple_of(cidx * c, c), c)
            qc, kc, vc = qs[hh, rows, :], ks[hh, rows, :], vs[hh, rows, :]
            kcb = kc.astype(BF16)
            kk = _bdot_nt(kcb, kcb)
            qkr = _bdot_nt(qc, kcb)
            gcl = gcol[hh, rows, :]
            grw = grow[hh, ui * (unit // cpl) + cc // cpl]
            lane0 = (cc % cpl) * c
            for d in range(2):
                incl, strict = masks[d]
                incl_t = masks[1 - d][0]
                g_c, b_c = gcl[:, d:d + 1], gcl[:, 2 + d:3 + d]
                g_r = grw[d:d + 1, lane0:lane0 + c]
                gcc = jnp.sum(jnp.where(incl, g_r, 0.0), axis=1, keepdims=True)
                gcr = jnp.sum(jnp.where(incl_t, g_c, 0.0), axis=0, keepdims=True)
                decay = jnp.where(incl, jnp.exp(jnp.where(incl, gcc - gcr, 0.0)), 0.0)
                egc = jnp.exp(gcc)
                g_last = jnp.sum(g_c, axis=0, keepdims=True)
                probs.append(dict(
                    d=d, cidx=cidx,
                    lm=jnp.where(strict, b_c * kk * decay, 0.0),
                    rhs=jnp.concatenate([vc * b_c, kc * (b_c * egc)], axis=-1),
                    qk=jnp.where(incl, qkr * decay, 0.0).astype(BF16),
                    kexp=(kc * jnp.exp(g_last - gcc)).astype(BF16),
                    qexp=qc * egc,
                    eg=jnp.broadcast_to(jnp.exp(g_last), (1, HEAD))))
        sols = _unit_tri_solves([p["lm"] for p in probs], [p["rhs"] for p in probs])
        sols = [s.astype(BF16) for s in sols]
        kts = [_bdot_tn(p["kexp"], s) for p, s in zip(probs, sols)]
        qos = [_bdot(p["qk"], s) for p, s in zip(probs, sols)]
        for p, kt, qo in zip(probs, kts, qos):
            d, cidx = p["d"], p["cidx"]
            b_s[hh, d, cidx] = kt[:, :HEAD]
            w_s[hh, d, cidx] = kt[:, HEAD:].astype(BF16)
            o_s[hh, d, cidx] = qo[:, :HEAD]
            q_s[hh, d, cidx] = (p["qexp"] - qo[:, HEAD:]).astype(BF16)
            e_s[hh, d, cidx] = p["eg"]
        return carry

    for hh in heads:
        lax.fori_loop(0, n_chunks // unit, functools.partial(phase_a, hh), 0)

    chains = [(hh, d) for hh in heads for d in range(2)]

    def phase_b(step, carry):
        nxt = []
        for (hh, d), s in zip(chains, carry):
            cidx = step if d == 0 else n_chunks - 1 - step
            sb = s.astype(BF16)
            st_s[hh, d, cidx] = sb
            nxt.append(e_s[hh, d, cidx] * s + b_s[hh, d, cidx]
                       - jnp.dot(w_s[hh, d, cidx], sb, preferred_element_type=F32))
        return tuple(nxt)

    if has_s0:
        s_init = tuple(s0_ref[0, 0, d, hh] for hh, d in chains)
    else:
        s_init = tuple(jnp.zeros((HEAD, HEAD), F32) for _ in chains)
    s_fin = lax.fori_loop(0, n_chunks, phase_b, s_init, unroll=n_chunks <= DELTA_UNIT)
    if so_ref is not None:
        for (hh, d), s in zip(chains, s_fin):
            so_ref[0, n_prev, d, hh] = s

    gain = gain_ref[...]

    def phase_c(hh, ui, carry):
        lanes = slice(hh * HEAD, (hh + 1) * HEAD)
        cids = [ui * unit + cc for cc in range(unit)]
        prods = [[jnp.dot(q_s[hh, d, cidx], st_s[hh, d, cidx], preferred_element_type=F32) for d in range(2)]
                 for cidx in cids]
        for cidx, (of, ob) in zip(cids, prods):
            rows = pl.ds(pl.multiple_of(cidx * c, c), c)
            o = (of + o_s[hh, 0, cidx]) + (ob + o_s[hh, 1, cidx])
            o_ref[rows, lanes] = _rms(o, gain) * _silu(z_ref[rows, lanes])
        return carry

    for hh in heads:
        lax.fori_loop(0, n_chunks // unit, functools.partial(phase_c, hh), 0)


def _delta(tok, p, ab, abt, par, conv_w, gain, ctx, j, state):
    n_heads = par.shape[-1]
    dc = n_heads * HEAD
    l, nseq = (tok.l_ctx, tok.n_ctx) if ctx else (tok.l_lat, tok.n_lat)
    row0 = 0 if ctx else tok.t_ctx // l
    nc = l // CHUNK
    flat = jnp.pad(jnp.concatenate([par[0].reshape(-1), par[1].reshape(-1)]).reshape(2, 2 * n_heads),
                   ((0, 0), (0, LANES - 2 * n_heads)))
    hb = math.gcd(DELTA_HEADS, n_heads)
    groups = n_heads // hb
    in_specs = ([pl.BlockSpec((l, hb * HEAD), lambda s, g, c=c: (row0 + s, c * groups + g)) for c in range(4)]
                + [pl.BlockSpec((CONV_K, hb * HEAD), lambda s, g, c=c: (0, c * groups + g)) for c in range(3)]
                + [pl.BlockSpec((l, LANES), lambda s, h: (row0 + s, 0)),
                   pl.BlockSpec((LANES, l), lambda s, h: (0, row0 + s)),
                   pl.BlockSpec((2, LANES), lambda s, h: (0, 0)),
                   pl.BlockSpec((LANES, 2), lambda s, h: (0, 0)),
                   pl.BlockSpec((1, HEAD), lambda s, h: (0, 0))])
    args = [p, p, p, p, conv_w, conv_w, conv_w, ab, abt, flat, flat.T, gain.reshape(1, HEAD)]
    out_specs = [pl.BlockSpec((l, hb * HEAD), lambda s, g: (s, g))]
    out_shape = [jax.ShapeDtypeStruct((nseq * l, dc), F32)]
    if ctx:
        out_specs.append(pl.BlockSpec((1, j + 1, 2, hb, HEAD, HEAD), lambda s, g: (s, 0, 0, g, 0, 0)))
        out_shape.append(jax.ShapeDtypeStruct((nseq, j + 1, 2, n_heads, HEAD, HEAD), F32))
        if j:
            in_specs.append(pl.BlockSpec((1, j, 2, hb, HEAD, HEAD), lambda s, g: (s, 0, 0, g, 0, 0)))
            args.append(state)
    else:
        in_specs.append(pl.BlockSpec((1, 1, 2, hb, HEAD, HEAD), lambda s, g: (s, j, 0, g, 0, 0)))
        args.append(state)
    return pl.pallas_call(
        functools.partial(_delta_kernel, not ctx, j if ctx else 0, n_heads),
        grid=(nseq, groups),
        in_specs=in_specs,
        out_specs=out_specs,
        out_shape=out_shape,
        scratch_shapes=[pltpu.VMEM((hb, l, HEAD), F32), pltpu.VMEM((hb, l, HEAD), F32), pltpu.VMEM((hb, l, HEAD), F32),
                        pltpu.VMEM((hb, l, 4), F32), pltpu.VMEM((hb, l // LANES, 4, LANES), F32),
                        pltpu.VMEM((l, LANES), F32), pltpu.VMEM((LANES, l), F32),
                        pltpu.VMEM((hb, 2, nc, HEAD, HEAD), F32), pltpu.VMEM((hb, 2, nc, HEAD, HEAD), BF16),
                        pltpu.VMEM((hb, 2, nc, CHUNK, HEAD), BF16), pltpu.VMEM((hb, 2, nc, CHUNK, HEAD), F32),
                        pltpu.VMEM((hb, 2, nc, 1, HEAD), F32), pltpu.VMEM((hb, 2, nc, HEAD, HEAD), BF16)],
        compiler_params=_cparams(("parallel", "arbitrary")),
        name="delta_ctx" if ctx else "delta_lat",
    )(*args)


def _router_kernel(h_ref, w_ref, b_ref, eid_ref, wt_ref, rank_ref, cnt_ref, tri_scr, run_scr):
    i = pl.program_id(0)
    tm = h_ref.shape[0]
    na = 2 * tm

    @pl.when(i == 0)
    def _():
        run_scr[...] = jnp.zeros_like(run_scr)
        a0 = lax.broadcasted_iota(jnp.int32, (na, na), 0)
        a1 = lax.broadcasted_iota(jnp.int32, (na, na), 1)
        tri_scr[...] = (a0 <= a1).astype(BF16)

    h = h_ref[...]
    w = w_ref[...]
    h_hi = h.astype(BF16)
    h_lo = (h - h_hi.astype(F32)).astype(BF16)
    w_hi = w.astype(BF16)
    w_lo = (w - w_hi.astype(F32)).astype(BF16)
    lt = _bdot_nt(w_hi, h_hi) + (_bdot_nt(w_lo, h_hi) + _bdot_nt(w_hi, h_lo)) + b_ref[...]

    def first_argmax(x, valid=None):
        rows = lax.broadcasted_iota(jnp.int32, x.shape, 0)
        if valid is not None:
            x = jnp.where(valid, x, -jnp.inf)
        m = jnp.max(x, axis=0, keepdims=True)
        idx = jnp.min(jnp.where(x == m, rows, x.shape[0]), axis=0, keepdims=True)
        return m, idx

    gl = lt[0:N_GROUPS]
    gmax, gidx = first_argmax(gl)
    g_w = 1.0 / jnp.sum(jnp.exp(gl - gmax), axis=0, keepdims=True)
    el = jnp.zeros((E_PER_GROUP, tm), F32)
    for g in range(N_GROUPS):
        lo = N_GROUPS + g * E_PER_GROUP
        el = jnp.where(gidx == g, lt[lo:lo + E_PER_GROUP], el)
    v1, i1 = first_argmax(el)
    rows = lax.broadcasted_iota(jnp.int32, el.shape, 0)
    v2, i2 = first_argmax(el, rows != i1)
    e21 = jnp.exp(v2 - v1)
    p1 = 1.0 / (1.0 + e21)
    e1 = gidx * E_PER_GROUP + i1
    e2 = gidx * E_PER_GROUP + i2
    eid_ref[...] = jnp.concatenate([e1, e2], axis=0)[None]
    wt_ref[...] = jnp.concatenate([g_w * p1, g_w * (e21 * p1)], axis=0)[None]

    e_all = jnp.concatenate([e1, e2], axis=1)
    onehot = lax.broadcasted_iota(jnp.int32, (N_EXPERTS, na), 0) == e_all
    csum = jnp.dot(onehot.astype(BF16), tri_scr[...], preferred_element_type=F32)
    run = run_scr[:, 0:1]
    rank = jnp.sum(jnp.where(onehot, csum + run, 0.0), axis=0, keepdims=True) - 1.0
    rank = rank.astype(jnp.int32)
    rank_ref[...] = jnp.concatenate([rank[:, :tm], rank[:, tm:]], axis=0)[None]
    run_new = run_scr[...] + csum[:, na - 1:na]
    run_scr[...] = run_new
    cnt_ref[...] = run_new


def _router(h, w_rt, b_rt, tm=ROW_TILE):
    t, d = h.shape
    nt = t // tm
    nr = w_rt.shape[0]
    tile_spec = pl.BlockSpec((1, 2, tm), lambda i: (i, 0, 0))
    return pl.pallas_call(
        _router_kernel,
        grid=(nt,),
        in_specs=[pl.BlockSpec((tm, d), lambda i: (i, 0)),
                  pl.BlockSpec((nr, d), lambda i: (0, 0)),
                  pl.BlockSpec((nr, 1), lambda i: (0, 0))],
        out_specs=[tile_spec, tile_spec, tile_spec,
                   pl.BlockSpec((N_EXPERTS, LANES), lambda i: (0, 0))],
        out_shape=[jax.ShapeDtypeStruct((nt, 2, tm), jnp.int32),
                   jax.ShapeDtypeStruct((nt, 2, tm), F32),
                   jax.ShapeDtypeStruct((nt, 2, tm), jnp.int32),
                   jax.ShapeDtypeStruct((N_EXPERTS, LANES), F32)],
        scratch_shapes=[pltpu.VMEM((2 * tm, 2 * tm), BF16), pltpu.VMEM((N_EXPERTS, LANES), F32)],
        compiler_params=_cparams(("arbitrary",)),
        name="moe_router",
    )(h, w_rt, b_rt)


def _dispatch_kernel(pos_ref, prev_ref, h_hbm, xs_in, xs_out, hbuf, load_sem, row_sem):
    del xs_in
    i = pl.program_id(0)
    n = pl.num_programs(0)
    tm = pos_ref.shape[2]

    def tile_load(tile):
        s = tile % 2
        return pltpu.make_async_copy(h_hbm.at[pl.ds(tile * tm, tm), :], hbuf.at[s], load_sem.at[s])

    def row_copy(p_ref, tile, k, r):
        s = tile % 2
        return pltpu.make_async_copy(hbuf.at[s, pl.ds(r, 1), :], xs_out.at[pl.ds(p_ref[0, k, r], 1), :],
                                     row_sem.at[s])

    def wait_rows(p_ref, tile):
        def wait(r, carry):
            row_copy(p_ref, tile, 0, r).wait()
            row_copy(p_ref, tile, 1, r).wait()
            return carry
        lax.fori_loop(0, tm, wait, 0, unroll=DMA_UNROLL)

    @pl.when(i == 0)
    def _():
        tile_load(0).start()

    tile_load(i).wait()

    def start(r, carry):
        row_copy(pos_ref, i, 0, r).start()
        row_copy(pos_ref, i, 1, r).start()
        return carry

    lax.fori_loop(0, tm, start, 0, unroll=DMA_UNROLL)

    @pl.when(i > 0)
    def _():
        wait_rows(prev_ref, i - 1)

    @pl.when(i + 1 < n)
    def _():
        tile_load(i + 1).start()

    @pl.when(i == n - 1)
    def _():
        wait_rows(pos_ref, i)


def _dispatch(h, pos, slots, tm=ROW_TILE):
    t, d = h.shape
    return pl.pallas_call(
        _dispatch_kernel,
        grid=(t // tm,),
        in_specs=[pl.BlockSpec((1, 2, tm), lambda i: (i, 0, 0), memory_space=pltpu.SMEM),
                  pl.BlockSpec((1, 2, tm), lambda i: (jnp.maximum(i - 1, 0), 0, 0), memory_space=pltpu.SMEM),
                  pl.BlockSpec(memory_space=pl.ANY),
                  pl.BlockSpec(memory_space=pl.ANY)],
        out_specs=pl.BlockSpec(memory_space=pl.ANY),
        out_shape=jax.ShapeDtypeStruct(slots.shape, F32),
        input_output_aliases={3: 0},
        scratch_shapes=[pltpu.VMEM((2, tm, d), F32), pltpu.SemaphoreType.DMA((2,)), pltpu.SemaphoreType.DMA((2,))],
        compiler_params=_cparams(("arbitrary",)),
        name="moe_dispatch",
    )(pos, pos, h, slots)


def _moe_kernel(tile_e_ref, n_used_ref, x_ref, w1_ref, w3_ref, w2_ref, y_ref, w1b, w3b, w2b):
    i = pl.program_id(0)
    used = i < n_used_ref[0]
    new_expert = (i == 0) | (tile_e_ref[i] != tile_e_ref[jnp.maximum(i - 1, 0)])

    @pl.when(used & new_expert)
    def _():
        w1b[...] = w1_ref[0, 0].astype(BF16)
        w3b[...] = w3_ref[0, 0].astype(BF16)
        w2b[...] = w2_ref[0, 0].astype(BF16)

    @pl.when(used)
    def _():
        x = x_ref[...].astype(BF16)
        a = jnp.dot(x, w1b[...], preferred_element_type=F32)
        b = jnp.dot(x, w3b[...], preferred_element_type=F32)
        hid = _silu(a) * b
        y_ref[...] = jnp.dot(hid.astype(BF16), w2b[...], preferred_element_type=F32)

    @pl.when(i >= n_used_ref[0])
    def _():
        y_ref[...] = jnp.zeros_like(y_ref)


def _moe_experts(xs, tile_e, n_used, w1, w3, w2, layer, tm=MOE_TILE):
    n_slots, d = xs.shape
    ff = w1.shape[3]

    def row_map(i, te, nu):
        return (jnp.minimum(i, nu[0] - 1), 0)

    def w_map(i, te, nu):
        return (layer, te[i], 0, 0)

    grid_spec = pltpu.PrefetchScalarGridSpec(
        num_scalar_prefetch=2,
        grid=(n_slots // tm,),
        in_specs=[pl.BlockSpec((tm, d), row_map),
                  pl.BlockSpec((1, 1, d, ff), w_map),
                  pl.BlockSpec((1, 1, d, ff), w_map),
                  pl.BlockSpec((1, 1, ff, d), w_map)],
        out_specs=pl.BlockSpec((tm, d), lambda i, te, nu: (i, 0)),
        scratch_shapes=[pltpu.VMEM((d, ff), BF16), pltpu.VMEM((d, ff), BF16), pltpu.VMEM((ff, d), BF16)],
    )
    return pl.pallas_call(
        _moe_kernel,
        grid_spec=grid_spec,
        out_shape=jax.ShapeDtypeStruct((n_slots, d), F32),
        compiler_params=_cparams(("arbitrary",)),
        name="moe_experts",
    )(tile_e, n_used, xs, w1, w3, w2)


def _combine_kernel(final, pos_ref, next_ref, ys_hbm, wt_ref, x_ref, gate_ref, fg_ref, o_ref, buf, sem):
    i = pl.program_id(0)
    tm = x_ref.shape[0]
    slot = i % 2

    def row_copy(p_ref, s, k, r):
        return pltpu.make_async_copy(ys_hbm.at[pl.ds(p_ref[0, k, r], 1), :], buf.at[s, k, pl.ds(r, 1), :], sem.at[s])

    def start_all(p_ref, s):
        def start(r, carry):
            row_copy(p_ref, s, 0, r).start()
            row_copy(p_ref, s, 1, r).start()
            return carry
        lax.fori_loop(0, tm, start, 0, unroll=DMA_UNROLL)

    @pl.when(i == 0)
    def _():
        start_all(pos_ref, 0)

    @pl.when(i + 1 < pl.num_programs(0))
    def _():
        start_all(next_ref, 1 - slot)

    def wait(r, carry):
        row_copy(pos_ref, slot, 0, r).wait()
        row_copy(pos_ref, slot, 1, r).wait()
        return carry

    lax.fori_loop(0, tm, wait, 0, unroll=DMA_UNROLL)
    wt = wt_ref[...]
    x = x_ref[...] + gate_ref[0] * (wt[:, 0:1] * buf[slot, 0] + wt[:, 1:2] * buf[slot, 1])
    o_ref[...] = _rms(x, fg_ref[...]) if final else x


def _moe_combine(tok, ys, pos, wt_col, x, modr, layer, final_gain, final, tile0, n_tiles, tm=ROW_TILE):
    d = x.shape[1]
    base = (layer * 6 + 5) * MAX_CONDS
    return pl.pallas_call(
        functools.partial(_combine_kernel, final),
        grid=(n_tiles,),
        in_specs=[pl.BlockSpec((1, 2, tm), lambda i: (tile0 + i, 0, 0), memory_space=pltpu.SMEM),
                  pl.BlockSpec((1, 2, tm), lambda i: (tile0 + jnp.minimum(i + 1, n_tiles - 1), 0, 0),
                               memory_space=pltpu.SMEM),
                  pl.BlockSpec(memory_space=pl.ANY),
                  pl.BlockSpec((tm, 2), lambda i: (tile0 + i, 0)),
                  pl.BlockSpec((tm, d), lambda i: (tile0 + i, 0)),
                  pl.BlockSpec((1, 1, d), lambda i: (base + tok.cond_row(tile0 + i, tm), 0, 0)),
                  pl.BlockSpec((1, d), lambda i: (0, 0))],
        out_specs=pl.BlockSpec((tm, d), lambda i: (i, 0)),
        out_shape=jax.ShapeDtypeStruct((n_tiles * tm, d), F32),
        scratch_shapes=[pltpu.VMEM((2, 2, tm, d), F32), pltpu.SemaphoreType.DMA((2,))],
        compiler_params=_cparams(("arbitrary",)),
        name="moe_combine",
    )(pos, pos, ys, wt_col, x, modr, final_gain.reshape(1, d))


def _slot_tables(eid, rank, counts, tm):
    n_tiles = eid.size // tm + N_EXPERTS
    tiles_per_e = (counts + tm - 1) // tm
    tile_end = jnp.cumsum(tiles_per_e)
    slot0 = ((tile_end - tiles_per_e) * tm).astype(jnp.int32)
    n_used = tile_end[-1:].astype(jnp.int32)
    tile_e = jnp.sum(jnp.arange(n_tiles)[:, None] >= tile_end[None, :], axis=1)
    tile_e = jnp.minimum(tile_e, N_EXPERTS - 1).astype(jnp.int32)
    experts = jnp.arange(N_EXPERTS, dtype=jnp.int32)
    pos = rank + jnp.sum(jnp.where(eid[..., None] == experts, slot0, 0), axis=-1)
    return tile_e, n_used, pos


def _moe(tok, h, x, slots, w_rt, b_rt, w1, w3, w2, modr, layer, final_gain, final):
    eid, wt, rank, cnt = _router(h, w_rt, b_rt)
    tile_e, n_used, pos = _slot_tables(eid, rank, cnt[:, 0].astype(jnp.int32), MOE_TILE)
    xs = _dispatch(h, pos, slots)
    ys = _moe_experts(xs, tile_e, n_used, w1, w3, w2, layer)
    wt_col = wt.transpose(0, 2, 1).reshape(-1, 2)
    n_ctx_tiles = tok.t_ctx // ROW_TILE
    n_all_tiles = tok.t // ROW_TILE
    comb = functools.partial(_moe_combine, tok, ys, pos, wt_col, x, modr, layer, final_gain, final)
    if final:
        return (comb(0, n_ctx_tiles), comb(n_ctx_tiles, n_all_tiles - n_ctx_tiles)), xs
    return comb(0, n_all_tiles), xs


def _pick_tile(n, candidates):
    for c in candidates:
        if n % c == 0:
            return c
    raise ValueError(f"no tile for {n}")


def kernel(x_prompt, x_sample, cache_attn_k, cache_attn_v, state_ret, state_delta, c, c_ctx, w_mod, b_mod, norm1, norm2, w_in_even, diff_lambda, subln_gain, ret_decay, ret_norm, w_out_even, w_in_odd, conv_w, dn_a_log, dn_dt_bias, dn_norm, w_out_odd, moe_w_group, moe_b_group, moe_w_router, moe_b_router, moe_w1, moe_w3, moe_w2, final_norm):
    n_ctx, l_ctx, d = x_prompt.shape
    n_lat, l_lat, _ = x_sample.shape
    depth = w_mod.shape[0]
    tok = _Tokens(n_ctx, l_ctx, n_lat, l_lat)
    assert 1 + n_lat <= MAX_CONDS and l_ctx % ROW_TILE == 0 and l_lat % ROW_TILE == 0
    h_a = w_in_even.shape[2] // (7 * HEAD)
    h_c = dn_a_log.shape[2]
    d_c = h_c * HEAD

    x = jnp.concatenate([x_prompt.reshape(-1, d), x_sample.reshape(-1, d)], axis=0)
    cond = jnp.zeros((MAX_CONDS, d), F32).at[0].set(c_ctx).at[1:1 + n_lat].set(c)
    mod = _modulation(cond, w_mod, b_mod)
    modr = mod.reshape(depth, MAX_CONDS, 6, d).transpose(0, 2, 1, 3).reshape(depth * 6 * MAX_CONDS, 1, d)

    rt_rows = 4 * SUBLANES
    w_rt = jnp.concatenate([moe_w_group, moe_w_router], axis=2).transpose(0, 2, 1)
    w_rt = jnp.pad(w_rt, ((0, 0), (0, rt_rows - w_rt.shape[1]), (0, 0)))
    b_rt = jnp.pad(jnp.concatenate([moe_b_group, moe_b_router], axis=1),
                   ((0, 0), (0, rt_rows - N_GROUPS - N_EXPERTS)))[:, :, None]

    tm_in = _pick_tile(math.gcd(tok.t_ctx, l_lat), (1024, 512, ROW_TILE))
    even_caches = None
    delta_states = None
    slots = jnp.zeros(((2 * tok.t) // MOE_TILE + N_EXPERTS) * MOE_TILE * d, F32).reshape(-1, d)
    for layer in range(depth):
        i = layer // 2
        if layer % 2 == 0:
            w_in = w_in_even[i].astype(BF16)
            p, = _in_proj(tok, x, norm1[layer], modr, layer, w_in, tm_in, _pick_tile(w_in.shape[1], (1024, 512)))
            prm = (diff_lambda[i], subln_gain[i], ret_norm[i], ret_decay[i])
            oa, orr, *even_caches = _even_ctx(tok, p, layer, i, *prm, even_caches)
            outs_ctx = [oa, orr]
            outs_lat = _even_lat(tok, p, layer, i, *prm, cache_attn_k, cache_attn_v, state_ret)
            w_out = w_out_even[i].astype(BF16)
            ws = [w_out[:h_a * HEAD], w_out[h_a * HEAD:]]
        else:
            w_in = w_in_odd[i]
            n_main = 4 * d_c
            w_ab = jnp.pad(w_in[:, n_main:], ((0, 0), (0, LANES - 4 * h_c))).astype(BF16)
            p, ab, abt = _in_proj(tok, x, norm1[layer], modr, layer, w_in[:, :n_main].astype(BF16), tm_in,
                                  _pick_tile(n_main, (1024, 512)), w_ab)
            par = jnp.stack([dn_a_log[i], dn_dt_bias[i]])
            dargs = (tok, p, ab, abt, par, conv_w[i], dn_norm[i])
            o, delta_states = _delta(*dargs, True, i, delta_states)
            outs_ctx = [o]
            outs_lat = _delta(*dargs, False, i, state_delta)
            ws = [w_out_odd[i].astype(BF16)]
        x, h2 = _out_proj(tok, outs_ctx, outs_lat, ws, x, norm2[layer], modr, layer)
        final = layer == depth - 1
        x, slots = _moe(tok, h2, x, slots, w_rt[layer], b_rt[layer], moe_w1, moe_w3, moe_w2, modr, layer,
                        final_norm, final)

    y_ctx, y_lat = x
    return (y_ctx.reshape(n_ctx, l_ctx, d), y_lat.reshape(n_lat, l_lat, d), *even_caches, delta_states)
```

```python
import functools
import math

import jax
import jax.numpy as jnp
from jax import lax
from jax.experimental import pallas as pl
from jax.experimental.pallas import tpu as pltpu

F32 = jnp.float32
BF16 = jnp.bfloat16
HI = lax.Precision.HIGHEST

GRID_W = 64
DA = 64
HEAD = 128
CONV_K = 5
CHUNK = 64
ROPE_BASE = 10000.0
N_GROUPS = 4
E_PER_GROUP = 4
N_EXPERTS = N_GROUPS * E_PER_GROUP
EPS = 1e-6

LANES = 128
SUBLANES = 8
MAX_CONDS = SUBLANES
ROW_TILE = 256
MOE_TILE = 256
DMA_UNROLL = 8
VMEM_LIMIT = 56 * 1024 * 1024


def _cparams(sem):
    return pltpu.CompilerParams(dimension_semantics=sem, vmem_limit_bytes=VMEM_LIMIT)


def _bdot(a, b):
    return jnp.dot(a.astype(BF16), b.astype(BF16), preferred_element_type=F32)


def _bdot_nt(a, b):
    return lax.dot_general(a.astype(BF16), b.astype(BF16), (((1,), (1,)), ((), ())),
                           preferred_element_type=F32)


def _bdot_tn(a, b):
    return lax.dot_general(a.astype(BF16), b.astype(BF16), (((0,), (0,)), ((), ())),
                           preferred_element_type=F32)


def _hdot(a, b):
    return jnp.dot(a, b, precision=HI, preferred_element_type=F32)


def _idot(a, b):
    return _bdot(a, b)


def _sigmoid(x):
    return 1.0 / (1.0 + jnp.exp(-x))


def _silu(x):
    return x * _sigmoid(x)


def _softplus(x):
    return jnp.maximum(x, 0.0) + jnp.log(1.0 + jnp.exp(-jnp.abs(x)))


def _rms(x, gain):
    return x * lax.rsqrt(jnp.mean(x * x, axis=-1, keepdims=True) + EPS) * gain


def _softmax(s):
    m = jnp.max(s, axis=-1, keepdims=True)
    e = jnp.exp(s - m)
    return e / jnp.sum(e, axis=-1, keepdims=True)


def _mod_kernel(c_ref, w_ref, b_ref, o_ref):
    c = c_ref[...]
    o_ref[0] = _bdot(_silu(c), w_ref[0]) + b_ref[0]


def _modulation(cond, w_mod, b_mod, tn=1024):
    depth, d, n = w_mod.shape
    return pl.pallas_call(
        _mod_kernel,
        grid=(depth, n // tn),
        in_specs=[pl.BlockSpec((MAX_CONDS, d), lambda l, j: (0, 0)),
                  pl.BlockSpec((1, d, tn), lambda l, j: (l, 0, j)),
                  pl.BlockSpec((1, 1, tn), lambda l, j: (l, 0, j))],
        out_specs=pl.BlockSpec((1, MAX_CONDS, tn), lambda l, j: (l, 0, j)),
        out_shape=jax.ShapeDtypeStruct((depth, MAX_CONDS, n), F32),
        compiler_params=_cparams(("parallel", "parallel")),
        name="adaln_mod",
    )(cond, w_mod, b_mod.reshape(depth, 1, n))


class _Tokens:
    def __init__(self, n_ctx, l_ctx, n_lat, l_lat):
        self.n_ctx, self.l_ctx, self.n_lat, self.l_lat = n_ctx, l_ctx, n_lat, l_lat
        self.t_ctx = n_ctx * l_ctx
        self.t = self.t_ctx + n_lat * l_lat

    def cond_row(self, i, tm):
        n0 = self.t_ctx // tm
        return jnp.where(i < n0, 0, 1 + (i - n0) // (self.l_lat // tm))


def _mod_spec(tok, tm, layer, which, d):
    base = (layer * 6 + which) * MAX_CONDS
    return pl.BlockSpec((1, 1, d), lambda i, *_: (base + tok.cond_row(i, tm), 0, 0))


def _inproj_kernel(n_side, x_ref, g_ref, sh_ref, sc_ref, w_ref, *rest):
    side_w = rest[:2 * n_side]
    o_ref = rest[2 * n_side]
    side_o = rest[2 * n_side + 1:-1]
    h_scr = rest[-1]

    @pl.when(pl.program_id(1) == 0)
    def _():
        h = _rms(x_ref[...], g_ref[...]) * (1.0 + sc_ref[0]) + sh_ref[0]
        hb = h.astype(BF16)
        h_scr[...] = hb
        for s in range(n_side):
            side_o[2 * s][...] = jnp.dot(hb, side_w[2 * s][...], preferred_element_type=F32)
            side_o[2 * s + 1][...] = lax.dot_general(side_w[2 * s + 1][...], hb, (((1,), (1,)), ((), ())),
                                                     preferred_element_type=F32)

    o_ref[...] = jnp.dot(h_scr[...], w_ref[...], preferred_element_type=F32)


def _in_proj(tok, x, gain, modr, layer, w, w_idx, n, tm, tn, side_w=None):
    t, d = x.shape
    n_side = 0 if side_w is None else 1
    in_specs = [pl.BlockSpec((tm, d), lambda i, j: (i, 0)),
                pl.BlockSpec((1, d), lambda i, j: (0, 0)),
                _mod_spec(tok, tm, layer, 0, d),
                _mod_spec(tok, tm, layer, 1, d),
                pl.BlockSpec((None, d, tn), lambda i, j: (w_idx, 0, j))]
    args = [x, gain.reshape(1, d), modr, modr, w]
    out_specs = [pl.BlockSpec((tm, tn), lambda i, j: (i, j))]
    out_shape = [jax.ShapeDtypeStruct((t, n), F32)]
    if n_side:
        in_specs += [pl.BlockSpec((d, LANES), lambda i, j: (0, 0)), pl.BlockSpec((LANES, d), lambda i, j: (0, 0))]
        args += [side_w, side_w.T]
        out_specs += [pl.BlockSpec((tm, LANES), lambda i, j: (i, 0)), pl.BlockSpec((LANES, tm), lambda i, j: (0, i))]
        out_shape += [jax.ShapeDtypeStruct((t, LANES), F32), jax.ShapeDtypeStruct((LANES, t), F32)]
    return pl.pallas_call(
        functools.partial(_inproj_kernel, n_side),
        grid=(t // tm, n // tn),
        in_specs=in_specs,
        out_specs=out_specs,
        out_shape=out_shape,
        scratch_shapes=[pltpu.VMEM((tm, d), BF16)],
        compiler_params=_cparams(("parallel", "arbitrary")),
        name="in_proj",
    )(*args)


def _outproj_kernel(n_in, n_ctx_tiles, *refs):
    oc_refs = refs[:n_in]
    ol_refs = refs[n_in:2 * n_in]
    w_refs = refs[2 * n_in:3 * n_in]
    x_ref, gate_ref, g2_ref, sh_ref, sc_ref, xo_ref, h_ref = refs[3 * n_in:]
    is_ctx = pl.program_id(0) < n_ctx_tiles
    acc = None
    for oc, ol, w_r in zip(oc_refs, ol_refs, w_refs):
        part = _bdot(jnp.where(is_ctx, oc[...], ol[...]), w_r[...])
        acc = part if acc is None else acc + part
    x = x_ref[...] + gate_ref[0] * acc
    xo_ref[...] = x
    h = _rms(x, g2_ref[...]) * (1.0 + sc_ref[0]) + sh_ref[0]
    h_ref[...] = h


def _out_proj(tok, outs_ctx, outs_lat, ws, x, gain2, modr, layer, tm=ROW_TILE):
    t, d = x.shape
    n_in = len(ws)
    n0 = tok.t_ctx // tm
    in_specs = ([pl.BlockSpec((tm, o.shape[1]), lambda i: (jnp.minimum(i, n0 - 1), 0)) for o in outs_ctx]
                + [pl.BlockSpec((tm, o.shape[1]), lambda i: (jnp.maximum(i - n0, 0), 0)) for o in outs_lat]
                + [pl.BlockSpec(w.shape, lambda i: (0, 0)) for w in ws]
                + [pl.BlockSpec((tm, d), lambda i: (i, 0)),
                   _mod_spec(tok, tm, layer, 2, d),
                   pl.BlockSpec((1, d), lambda i: (0, 0)),
                   _mod_spec(tok, tm, layer, 3, d),
                   _mod_spec(tok, tm, layer, 4, d)])
    return pl.pallas_call(
        functools.partial(_outproj_kernel, n_in, n0),
        grid=(t // tm,),
        in_specs=in_specs,
        out_specs=[pl.BlockSpec((tm, d), lambda i: (i, 0)),
                   pl.BlockSpec((tm, d), lambda i: (i, 0))],
        out_shape=[jax.ShapeDtypeStruct((t, d), F32), jax.ShapeDtypeStruct((t, d), F32)],
        compiler_params=_cparams(("parallel",)),
        name="out_proj",
    )(*outs_ctx, *outs_lat, *ws, x, modr, gain2.reshape(1, d), modr, modr)


def _lambda(dl_ref, lam_init):
    dl = dl_ref[0]
    s1 = jnp.sum(dl[0:1] * dl[1:2], axis=-1, keepdims=True)
    s2 = jnp.sum(dl[2:3] * dl[3:4], axis=-1, keepdims=True)
    return jnp.exp(s1) - jnp.exp(s2) + lam_init


def _log_gammas(rdec_ref):
    lg = -_softplus(-rdec_ref[0])
    return lg[0:1, 0:1], lg[1:2, 0:1]


def _diff_attn(q, k1, k2, v, lam):
    p1 = _softmax(_bdot_nt(q[:, :DA], k1) * (DA ** -0.5))
    p2 = _softmax(_bdot_nt(q[:, DA:], k2) * (DA ** -0.5))
    return _bdot(p1 - lam * p2, v)


def _ret_mask(row0, nq, nk, lg_f, lg_b):
    qi = lax.broadcasted_iota(jnp.int32, (nq, nk), 0) + row0
    kj = lax.broadcasted_iota(jnp.int32, (nq, nk), 1)
    diff = (qi - kj).astype(F32)
    fwd = jnp.where(diff >= 0, jnp.exp(jnp.maximum(diff, 0.0) * lg_f), 0.0)
    bwd = jnp.where(diff <= 0, jnp.exp(jnp.maximum(-diff, 0.0) * lg_b), 0.0)
    return fwd + bwd


def _even_ctx_kernel(lam_init, n_prev, qa_ref, ka_ref, va_ref, qb_ref, kb_ref, vb_ref, gb_ref,
                     dl_ref, subln_ref, retn_ref, rdec_ref, *rest):
    if n_prev:
        kp_ref, vp_ref, sp_ref = rest[:3]
        rest = rest[3:]
    oa_ref, or_ref, ko_ref, vo_ref, so_ref = rest
    if n_prev:
        ko_ref[0, 0:n_prev] = kp_ref[0]
        vo_ref[0, 0:n_prev] = vp_ref[0]
        so_ref[0, 0:n_prev] = sp_ref[0]
    n = qa_ref.shape[0]
    heads = range(qa_ref.shape[1] // HEAD)
    lam = _lambda(dl_ref, lam_init)

    def cols(ref, hh):
        return ref[:, hh * HEAD:(hh + 1) * HEAD]

    ka = [cols(ka_ref, hh) for hh in heads]
    va = [cols(va_ref, hh) for hh in heads]
    for hh in heads:
        ko_ref[0, n_prev, hh] = ka[hh]
        vo_ref[0, n_prev, hh] = va[hh]
    kab = [x.astype(BF16) for x in ka]
    qa = [cols(qa_ref, hh) for hh in heads]
    s1 = [_bdot_nt(qa[hh][:, :DA], kab[hh][:, :DA]) * (DA ** -0.5) for hh in heads]
    s2 = [_bdot_nt(qa[hh][:, DA:], kab[hh][:, DA:]) * (DA ** -0.5) for hh in heads]

    kb = [cols(kb_ref, hh) * (HEAD ** -0.5) for hh in heads]
    vb = [cols(vb_ref, hh).astype(BF16) for hh in heads]
    sr = [_bdot_nt(cols(qb_ref, hh), kb[hh]) for hh in heads]

    att = [_softmax(s1[hh]) - lam * _softmax(s2[hh]) for hh in heads]
    oa = [_bdot(att[hh], va[hh].astype(BF16)) for hh in heads]
    lgs = [_log_gammas(rdec_ref.at[pl.ds(hh, 1)]) for hh in heads]
    ret = [sr[hh] * _ret_mask(0, n, n, *lgs[hh]) for hh in heads]
    orr = [_bdot(ret[hh], vb[hh]) for hh in heads]
    pos = lax.broadcasted_iota(jnp.int32, (n, 1), 0).astype(F32)
    sf = [_bdot_tn(kb[hh] * jnp.exp((n - 1.0 - pos) * lgs[hh][0]), vb[hh]) for hh in heads]
    sb = [_bdot_tn(kb[hh] * jnp.exp(pos * lgs[hh][1]), vb[hh]) for hh in heads]
    for hh in heads:
        lanes = slice(hh * HEAD, (hh + 1) * HEAD)
        oa_ref[:, lanes] = _rms(oa[hh], subln_ref[...]) * (1.0 - lam_init)
        or_ref[:, lanes] = _rms(orr[hh], retn_ref[...]) * _silu(cols(gb_ref, hh))
        so_ref[0, n_prev, 0, hh] = sf[hh]
        so_ref[0, n_prev, 1, hh] = sb[hh]


EVEN_HEADS = 8


def _head_spec(rows, col0, row0=0):
    return pl.BlockSpec((rows, HEAD), lambda s, h: (row0 + s, col0 + h))


def _even_param_specs():
    return [pl.BlockSpec((1, 4, DA), lambda s, h: (0, 0, 0)),
            pl.BlockSpec((1, HEAD), lambda s, h: (0, 0)),
            pl.BlockSpec((1, HEAD), lambda s, h: (0, 0)),
            pl.BlockSpec((1, 2, LANES), lambda s, h: (h, 0, 0))]


def _even_params(dl, subln, retn, rdec):
    n_heads = rdec.shape[1]
    rdec_b = jnp.broadcast_to(rdec.T[:, :, None], (n_heads, 2, LANES))
    return dl.reshape(1, 4, DA), subln.reshape(1, HEAD), retn.reshape(1, HEAD), rdec_b


def _even_ctx(tok, p, layer, i, dl, subln, retn, rdec, prev):
    n_heads = p.shape[1] // (7 * HEAD)
    lam_init = 0.8 - 0.6 * math.exp(-0.3 * layer)
    l, nseq = tok.l_ctx, tok.n_ctx
    dh = n_heads * HEAD
    hb = math.gcd(EVEN_HEADS, n_heads)
    groups = n_heads // hb
    args = [p] * 7 + list(_even_params(dl, subln, retn, rdec))
    in_specs = ([pl.BlockSpec((l, hb * HEAD), lambda s, g, c=c: (s, c * groups + g)) for c in range(7)]
                + _even_param_specs()[:3] + [pl.BlockSpec((hb, 2, LANES), lambda s, g: (g, 0, 0))])
    if i:
        in_specs += [pl.BlockSpec((1, i, hb, l, HEAD), lambda s, g: (s, 0, g, 0, 0)),
                     pl.BlockSpec((1, i, hb, l, HEAD), lambda s, g: (s, 0, g, 0, 0)),
                     pl.BlockSpec((1, i, 2, hb, HEAD, HEAD), lambda s, g: (s, 0, 0, g, 0, 0))]
        args += list(prev)
    return pl.pallas_call(
        functools.partial(_even_ctx_kernel, lam_init, i),
        grid=(nseq, groups),
        in_specs=in_specs,
        out_specs=[pl.BlockSpec((l, hb * HEAD), lambda s, g: (s, g)),
                   pl.BlockSpec((l, hb * HEAD), lambda s, g: (s, g)),
                   pl.BlockSpec((1, i + 1, hb, l, HEAD), lambda s, g: (s, 0, g, 0, 0)),
                   pl.BlockSpec((1, i + 1, hb, l, HEAD), lambda s, g: (s, 0, g, 0, 0)),
                   pl.BlockSpec((1, i + 1, 2, hb, HEAD, HEAD), lambda s, g: (s, 0, 0, g, 0, 0))],
        out_shape=[jax.ShapeDtypeStruct((tok.t_ctx, dh), F32),
                   jax.ShapeDtypeStruct((tok.t_ctx, dh), F32),
                   jax.ShapeDtypeStruct((nseq, i + 1, n_heads, l, HEAD), F32),
                   jax.ShapeDtypeStruct((nseq, i + 1, n_heads, l, HEAD), F32),
                   jax.ShapeDtypeStruct((nseq, i + 1, 2, n_heads, HEAD, HEAD), F32)],
        compiler_params=_cparams(("parallel", "parallel")),
        name="even_mixer_ctx",
    )(*args)


def _rope(x, cos, sin_signed):
    half = DA // 2
    lane = lax.broadcasted_iota(jnp.int32, x.shape, 1)
    swapped = jnp.where((lane % DA) < half,
                        pltpu.roll(x, x.shape[1] - half, axis=1),
                        pltpu.roll(x, half, axis=1))
    return x * cos + swapped * sin_signed


def _even_lat_kernel(lam_init, tq, qa_ref, ka_ref, va_ref, qb_ref, kb_ref, vb_ref, gb_ref,
                     ck_ref, cv_ref, s0_ref, cos_ref, sin_ref,
                     dl_ref, subln_ref, retn_ref, rdec_ref,
                     oa_ref, or_ref, kall, vall):
    n = qa_ref.shape[0]
    past = ck_ref.shape[3]
    lam = _lambda(dl_ref, lam_init)
    cos = cos_ref[...]
    sin = sin_ref[...]
    kall[0:past, :] = ck_ref[0, 0, 0].astype(BF16)
    kall[past:past + n, :] = _rope(ka_ref[...], cos, sin).astype(BF16)
    vall[0:past, :] = cv_ref[0, 0, 0].astype(BF16)
    vall[past:past + n, :] = va_ref[...].astype(BF16)

    lg_f, lg_b = _log_gammas(rdec_ref)
    kb = (kb_ref[...] * (HEAD ** -0.5)).astype(BF16)
    vb = vb_ref[...].astype(BF16)
    s0f = s0_ref[0, 0, 0, 0].astype(BF16)
    s0b = s0_ref[0, 0, 1, 0].astype(BF16)
    for blk in range(n // tq):
        rows = pl.ds(blk * tq, tq)
        q = _rope(qa_ref[rows, :], cos[blk * tq:(blk + 1) * tq], sin[blk * tq:(blk + 1) * tq])
        o = _diff_attn(q, kall[:, :DA], kall[:, DA:], vall[...], lam)
        oa_ref[rows, :] = _rms(o, subln_ref[...]) * (1.0 - lam_init)

        qb = qb_ref[rows, :]
        a = _bdot_nt(qb, kb) * _ret_mask(blk * tq, tq, n, lg_f, lg_b)
        pos = (lax.broadcasted_iota(jnp.int32, (tq, 1), 0) + blk * tq).astype(F32)
        o = (_bdot(a, vb) + _bdot(qb * jnp.exp((pos + 1.0) * lg_f), s0f)
             + _bdot(qb * jnp.exp((n - pos) * lg_b), s0b))
        or_ref[rows, :] = _rms(o, retn_ref[...]) * _silu(gb_ref[rows, :])


def _rope_tables(n_tok):
    n_freq = DA // 4
    t = jnp.arange(n_tok)
    inv_freq = ROPE_BASE ** (-jnp.arange(n_freq, dtype=F32) / n_freq)
    ang = jnp.concatenate([(t // GRID_W).astype(F32)[:, None] * inv_freq,
                           (t % GRID_W).astype(F32)[:, None] * inv_freq], axis=-1)
    cos, sin = jnp.cos(ang), jnp.sin(ang)
    return jnp.tile(cos, (1, 4)), jnp.tile(jnp.concatenate([-sin, sin], axis=-1), (1, 2))


def _even_lat(tok, p, layer, i, dl, subln, retn, rdec, cache_k, cache_v, state_ret, tq=256):
    n_heads = p.shape[1] // (7 * HEAD)
    lam_init = 0.8 - 0.6 * math.exp(-0.3 * layer)
    l, nseq = tok.l_lat, tok.n_lat
    row0 = tok.t_ctx // l
    past = cache_k.shape[3]
    dh = n_heads * HEAD
    cos, sin = _rope_tables(l)
    args = [p] * 7 + [cache_k, cache_v, state_ret, cos, sin] + list(_even_params(dl, subln, retn, rdec))
    return pl.pallas_call(
        functools.partial(_even_lat_kernel, lam_init, tq),
        grid=(nseq, n_heads),
        in_specs=[_head_spec(l, c * n_heads, row0) for c in range(7)]
        + [pl.BlockSpec((1, 1, 1, past, HEAD), lambda s, h: (s, i, h, 0, 0)),
           pl.BlockSpec((1, 1, 1, past, HEAD), lambda s, h: (s, i, h, 0, 0)),
           pl.BlockSpec((1, 1, 2, 1, HEAD, HEAD), lambda s, h: (s, i, 0, h, 0, 0)),
           pl.BlockSpec((l, HEAD), lambda s, h: (0, 0)),
           pl.BlockSpec((l, HEAD), lambda s, h: (0, 0))]
        + _even_param_specs(),
        out_specs=[pl.BlockSpec((l, HEAD), lambda s, h: (s, h)),
                   pl.BlockSpec((l, HEAD), lambda s, h: (s, h))],
        out_shape=[jax.ShapeDtypeStruct((nseq * l, dh), F32), jax.ShapeDtypeStruct((nseq * l, dh), F32)],
        scratch_shapes=[pltpu.VMEM((past + l, HEAD), BF16), pltpu.VMEM((past + l, HEAD), BF16)],
        compiler_params=_cparams(("parallel", "parallel")),
        name="even_mixer_lat",
    )(*args)


def _conv_silu(x, w):
    n = x.shape[0]
    pad = (CONV_K - 1) // 2
    row = lax.broadcasted_iota(jnp.int32, x.shape, 0)
    acc = x * w[pad:pad + 1]
    for j in range(CONV_K):
        d = j - pad
        if d == 0:
            continue
        shifted = pltpu.roll(x, (-d) % n, axis=0)
        valid = (row + d >= 0) & (row + d < n)
        acc = acc + jnp.where(valid, shifted, 0.0) * w[j:j + 1]
    return _silu(acc)


def _l2n(x):
    return x * lax.rsqrt(jnp.sum(x * x, axis=-1, keepdims=True) + EPS)


TRI_BLOCK = 16


def _unit_tri_solves(lms, rhss):
    c = lms[0].shape[0]
    ri = lax.broadcasted_iota(jnp.int32, (c, c), 0)
    ci = lax.broadcasted_iota(jnp.int32, (c, c), 1)
    blk = TRI_BLOCK
    same = (ri // blk) == (ci // blk)
    pws = [jnp.where(same, -lm, 0.0) for lm in lms]
    ns = list(pws)
    span = 1
    while 2 * span < blk:
        pws = [_idot(pw, pw) for pw in pws]
        ns = [n + pw + _idot(n, pw) for n, pw in zip(ns, pws)]
        span *= 2
    while blk < c:
        pair = ((ri // (2 * blk)) == (ci // (2 * blk))) & ((ri // blk) != (ci // blk))
        cs = [jnp.where(pair, lm, 0.0) for lm in lms]
        xs = [cm + _idot(n, cm) for n, cm in zip(ns, cs)]
        ns = [n - (x + _idot(x, n)) for n, x in zip(ns, xs)]
        blk *= 2
    return [rhs + _idot(n, rhs) for n, rhs in zip(ns, rhss)]


DELTA_UNIT = 4
DELTA_HEADS = 2


def _delta_kernel(has_s0, n_prev, n_heads, q_ref, k_ref, v_ref, z_ref, wq_ref, wk_ref, wv_ref,
                  gc_ref, gr_ref, parr_ref, parc_ref, gain_ref, *rest):
    s0_ref = so_ref = None
    if has_s0:
        s0_ref, o_ref = rest[:2]
        rest = rest[2:]
    else:
        if n_prev:
            sp_ref, rest = rest[0], rest[1:]
        o_ref, so_ref = rest[:2]
        rest = rest[2:]
        if n_prev:
            so_ref[0, 0:n_prev] = sp_ref[0]
    qs, ks, vs, gcol, grow, gf_s, gt_s, b_s, w_s, q_s, o_s, e_s, st_s = rest
    n = q_ref.shape[0]
    n_chunks = n // CHUNK
    c = CHUNK
    grp = pl.program_id(1)
    heads = range(q_ref.shape[1] // HEAD)

    def cols(ref, hh):
        return ref[:, hh * HEAD:(hh + 1) * HEAD]

    for hh in heads:
        qs[hh] = _l2n(_conv_silu(cols(q_ref, hh), cols(wq_ref, hh))) * (HEAD ** -0.5)
        ks[hh] = _l2n(_conv_silu(cols(k_ref, hh), cols(wk_ref, hh)))
        vs[hh] = _conv_silu(cols(v_ref, hh), cols(wv_ref, hh))

    lane = lax.broadcasted_iota(jnp.int32, gc_ref.shape, 1)

    @pl.when(grp == 0)
    def _():
        xc = gc_ref[...]
        pr = parr_ref[...]
        gf_s[...] = jnp.where(lane < 2 * n_heads, -jnp.exp(pr[0:1]) * _softplus(xc + pr[1:2]), _sigmoid(xc))
        xr = gr_ref[...]
        sub = lax.broadcasted_iota(jnp.int32, xr.shape, 0)
        pc = parc_ref[...]
        gt_s[...] = jnp.where(sub < 2 * n_heads, -jnp.exp(pc[:, 0:1]) * _softplus(xr + pc[:, 1:2]), _sigmoid(xr))

    gfull = gf_s[...]
    for hh in heads:
        head = grp * len(heads) + hh
        chans = [jnp.sum(jnp.where(lane == r * n_heads + head, gfull, 0.0), axis=1, keepdims=True) for r in range(4)]
        gcol[hh] = jnp.concatenate(chans, axis=1)
        for r in range(4):
            row = gt_s[pl.ds(r * n_heads + head, 1), :]
            for j in range(n // LANES):
                grow[hh, j, r:r + 1, :] = row[:, j * LANES:(j + 1) * LANES]

    ri = lax.broadcasted_iota(jnp.int32, (c, c), 0)
    ci = lax.broadcasted_iota(jnp.int32, (c, c), 1)
    masks = [(ri >= ci, ri > ci), (ri <= ci, ri < ci)]

    unit = min(DELTA_UNIT, n_chunks)
    cpl = LANES // c
    assert unit % cpl == 0 and n_chunks % unit == 0

    def phase_a(hh, ui, carry):
        probs = []
        for cc in range(unit):
            cidx = ui * unit + cc
            rows = pl.ds(pl.multiple_of(cidx * c, c), c)
            qc, kc, vc = qs[hh, rows, :], ks[hh, rows, :], vs[hh, rows, :]
            kcb = kc.astype(BF16)
            kk = _bdot_nt(kcb, kcb)
            qkr = _bdot_nt(qc, kcb)
            gcl = gcol[hh, rows, :]
            grw = grow[hh, ui * (unit // cpl) + cc // cpl]
            lane0 = (cc % cpl) * c
            for d in range(2):
                incl, strict = masks[d]
                incl_t = masks[1 - d][0]
                g_c, b_c = gcl[:, d:d + 1], gcl[:, 2 + d:3 + d]
                g_r = grw[d:d + 1, lane0:lane0 + c]
                gcc = jnp.sum(jnp.where(incl, g_r, 0.0), axis=1, keepdims=True)
                gcr = jnp.sum(jnp.where(incl_t, g_c, 0.0), axis=0, keepdims=True)
                decay = jnp.where(incl, jnp.exp(jnp.where(incl, gcc - gcr, 0.0)), 0.0)
                egc = jnp.exp(gcc)
                g_last = jnp.sum(g_c, axis=0, keepdims=True)
                probs.append(dict(
                    d=d, cidx=cidx,
                    lm=jnp.where(strict, b_c * kk * decay, 0.0),
                    rhs=jnp.concatenate([vc * b_c, kc * (b_c * egc)], axis=-1),
                    qk=jnp.where(incl, qkr * decay, 0.0).astype(BF16),
                    kexp=(kc * jnp.exp(g_last - gcc)).astype(BF16),
                    qexp=qc * egc,
                    eg=jnp.broadcast_to(jnp.exp(g_last), (1, HEAD))))
        sols = _unit_tri_solves([p["lm"] for p in probs], [p["rhs"] for p in probs])
        sols = [s.astype(BF16) for s in sols]
        kts = [_bdot_tn(p["kexp"], s) for p, s in zip(probs, sols)]
        qos = [_bdot(p["qk"], s) for p, s in zip(probs, sols)]
        for p, kt, qo in zip(probs, kts, qos):
            d, cidx = p["d"], p["cidx"]
            b_s[hh, d, cidx] = kt[:, :HEAD]
            w_s[hh, d, cidx] = kt[:, HEAD:].astype(BF16)
            o_s[hh, d, cidx] = qo[:, :HEAD]
            q_s[hh, d, cidx] = (p["qexp"] - qo[:, HEAD:]).astype(BF16)
            e_s[hh, d, cidx] = p["eg"]
        return carry

    for hh in heads:
        lax.fori_loop(0, n_chunks // unit, functools.partial(phase_a, hh), 0)

    chains = [(hh, d) for hh in heads for d in range(2)]

    def phase_b(step, carry):
        nxt = []
        for (hh, d), s in zip(chains, carry):
            cidx = step if d == 0 else n_chunks - 1 - step
            sb = s.astype(BF16)
            st_s[hh, d, cidx] = sb
            nxt.append(e_s[hh, d, cidx] * s + b_s[hh, d, cidx]
                       - jnp.dot(w_s[hh, d, cidx], sb, preferred_element_type=F32))
        return tuple(nxt)

    if has_s0:
        s_init = tuple(s0_ref[0, 0, d, hh] for hh, d in chains)
    else:
        s_init = tuple(jnp.zeros((HEAD, HEAD), F32) for _ in chains)
    s_fin = lax.fori_loop(0, n_chunks, phase_b, s_init, unroll=n_chunks <= DELTA_UNIT)
    if so_ref is not None:
        for (hh, d), s in zip(chains, s_fin):
            so_ref[0, n_prev, d, hh] = s

    gain = gain_ref[...]

    def phase_c(hh, ui, carry):
        lanes = slice(hh * HEAD, (hh + 1) * HEAD)
        cids = [ui * unit + cc for cc in range(unit)]
        prods = [[jnp.dot(q_s[hh, d, cidx], st_s[hh, d, cidx], preferred_element_type=F32) for d in range(2)]
                 for cidx in cids]
        for cidx, (of, ob) in zip(cids, prods):
            rows = pl.ds(pl.multiple_of(cidx * c, c), c)
            o = (of + o_s[hh, 0, cidx]) + (ob + o_s[hh, 1, cidx])
            o_ref[rows, lanes] = _rms(o, gain) * _silu(z_ref[rows, lanes])
        return carry

    for hh in heads:
        lax.fori_loop(0, n_chunks // unit, functools.partial(phase_c, hh), 0)


def _delta(tok, p, ab, abt, par, conv_w, gain, ctx, j, state):
    n_heads = par.shape[-1]
    dc = n_heads * HEAD
    l, nseq = (tok.l_ctx, tok.n_ctx) if ctx else (tok.l_lat, tok.n_lat)
    row0 = 0 if ctx else tok.t_ctx // l
    nc = l // CHUNK
    flat = jnp.pad(jnp.concatenate([par[0].reshape(-1), par[1].reshape(-1)]).reshape(2, 2 * n_heads),
                   ((0, 0), (0, LANES - 2 * n_heads)))
    hb = math.gcd(DELTA_HEADS, n_heads)
    groups = n_heads // hb
    in_specs = ([pl.BlockSpec((l, hb * HEAD), lambda s, g, c=c: (row0 + s, c * groups + g)) for c in range(4)]
                + [pl.BlockSpec((CONV_K, hb * HEAD), lambda s, g, c=c: (0, c * groups + g)) for c in range(3)]
                + [pl.BlockSpec((l, LANES), lambda s, h: (row0 + s, 0)),
                   pl.BlockSpec((LANES, l), lambda s, h: (0, row0 + s)),
                   pl.BlockSpec((2, LANES), lambda s, h: (0, 0)),
                   pl.BlockSpec((LANES, 2), lambda s, h: (0, 0)),
                   pl.BlockSpec((1, HEAD), lambda s, h: (0, 0))])
    args = [p, p, p, p, conv_w, conv_w, conv_w, ab, abt, flat, flat.T, gain.reshape(1, HEAD)]
    out_specs = [pl.BlockSpec((l, hb * HEAD), lambda s, g: (s, g))]
    out_shape = [jax.ShapeDtypeStruct((nseq * l, dc), F32)]
    if ctx:
        out_specs.append(pl.BlockSpec((1, j + 1, 2, hb, HEAD, HEAD), lambda s, g: (s, 0, 0, g, 0, 0)))
        out_shape.append(jax.ShapeDtypeStruct((nseq, j + 1, 2, n_heads, HEAD, HEAD), F32))
        if j:
            in_specs.append(pl.BlockSpec((1, j, 2, hb, HEAD, HEAD), lambda s, g: (s, 0, 0, g, 0, 0)))
            args.append(state)
    else:
        in_specs.append(pl.BlockSpec((1, 1, 2, hb, HEAD, HEAD), lambda s, g: (s, j, 0, g, 0, 0)))
        args.append(state)
    return pl.pallas_call(
        functools.partial(_delta_kernel, not ctx, j if ctx else 0, n_heads),
        grid=(nseq, groups),
        in_specs=in_specs,
        out_specs=out_specs,
        out_shape=out_shape,
        scratch_shapes=[pltpu.VMEM((hb, l, HEAD), F32), pltpu.VMEM((hb, l, HEAD), F32), pltpu.VMEM((hb, l, HEAD), F32),
                        pltpu.VMEM((hb, l, 4), F32), pltpu.VMEM((hb, l // LANES, 4, LANES), F32),
                        pltpu.VMEM((l, LANES), F32), pltpu.VMEM((LANES, l), F32),
                        pltpu.VMEM((hb, 2, nc, HEAD, HEAD), F32), pltpu.VMEM((hb, 2, nc, HEAD, HEAD), BF16),
                        pltpu.VMEM((hb, 2, nc, CHUNK, HEAD), BF16), pltpu.VMEM((hb, 2, nc, CHUNK, HEAD), F32),
                        pltpu.VMEM((hb, 2, nc, 1, HEAD), F32), pltpu.VMEM((hb, 2, nc, HEAD, HEAD), BF16)],
        compiler_params=_cparams(("parallel", "arbitrary")),
        name="delta_ctx" if ctx else "delta_lat",
    )(*args)


def _router_kernel(h_ref, w_ref, b_ref, eid_ref, wt_ref, rank_ref, cnt_ref, tri_scr, run_scr):
    i = pl.program_id(0)
    tm = h_ref.shape[0]
    na = 2 * tm

    @pl.when(i == 0)
    def _():
        run_scr[...] = jnp.zeros_like(run_scr)
        a0 = lax.broadcasted_iota(jnp.int32, (na, na), 0)
        a1 = lax.broadcasted_iota(jnp.int32, (na, na), 1)
        tri_scr[...] = (a0 <= a1).astype(BF16)

    h = h_ref[...]
    w = w_ref[...]
    h_hi = h.astype(BF16)
    h_lo = (h - h_hi.astype(F32)).astype(BF16)
    w_hi = w.astype(BF16)
    w_lo = (w - w_hi.astype(F32)).astype(BF16)
    lt = _bdot_nt(w_hi, h_hi) + (_bdot_nt(w_lo, h_hi) + _bdot_nt(w_hi, h_lo)) + b_ref[...]

    def first_argmax(x, valid=None):
        rows = lax.broadcasted_iota(jnp.int32, x.shape, 0)
        if valid is not None:
            x = jnp.where(valid, x, -jnp.inf)
        m = jnp.max(x, axis=0, keepdims=True)
        idx = jnp.min(jnp.where(x == m, rows, x.shape[0]), axis=0, keepdims=True)
        return m, idx

    gl = lt[0:N_GROUPS]
    gmax, gidx = first_argmax(gl)
    g_w = 1.0 / jnp.sum(jnp.exp(gl - gmax), axis=0, keepdims=True)
    el = jnp.zeros((E_PER_GROUP, tm), F32)
    for g in range(N_GROUPS):
        lo = N_GROUPS + g * E_PER_GROUP
        el = jnp.where(gidx == g, lt[lo:lo + E_PER_GROUP], el)
    v1, i1 = first_argmax(el)
    rows = lax.broadcasted_iota(jnp.int32, el.shape, 0)
    v2, i2 = first_argmax(el, rows != i1)
    e21 = jnp.exp(v2 - v1)
    p1 = 1.0 / (1.0 + e21)
    e1 = gidx * E_PER_GROUP + i1
    e2 = gidx * E_PER_GROUP + i2
    eid_ref[...] = jnp.concatenate([e1, e2], axis=0)[None]
    wt_ref[...] = jnp.concatenate([g_w * p1, g_w * (e21 * p1)], axis=0)[None]

    e_all = jnp.concatenate([e1, e2], axis=1)
    onehot = lax.broadcasted_iota(jnp.int32, (N_EXPERTS, na), 0) == e_all
    csum = jnp.dot(onehot.astype(BF16), tri_scr[...], preferred_element_type=F32)
    run = run_scr[:, 0:1]
    rank = jnp.sum(jnp.where(onehot, csum + run, 0.0), axis=0, keepdims=True) - 1.0
    rank = rank.astype(jnp.int32)
    rank_ref[...] = jnp.concatenate([rank[:, :tm], rank[:, tm:]], axis=0)[None]
    run_new = run_scr[...] + csum[:, na - 1:na]
    run_scr[...] = run_new
    cnt_ref[...] = run_new


def _router(h, w_rt, b_rt, tm=ROW_TILE):
    t, d = h.shape
    nt = t // tm
    nr = w_rt.shape[0]
    tile_spec = pl.BlockSpec((1, 2, tm), lambda i: (i, 0, 0))
    return pl.pallas_call(
        _router_kernel,
        grid=(nt,),
        in_specs=[pl.BlockSpec((tm, d), lambda i: (i, 0)),
                  pl.BlockSpec((nr, d), lambda i: (0, 0)),
                  pl.BlockSpec((nr, 1), lambda i: (0, 0))],
        out_specs=[tile_spec, tile_spec, tile_spec,
                   pl.BlockSpec((N_EXPERTS, LANES), lambda i: (0, 0))],
        out_shape=[jax.ShapeDtypeStruct((nt, 2, tm), jnp.int32),
                   jax.ShapeDtypeStruct((nt, 2, tm), F32),
                   jax.ShapeDtypeStruct((nt, 2, tm), jnp.int32),
                   jax.ShapeDtypeStruct((N_EXPERTS, LANES), F32)],
        scratch_shapes=[pltpu.VMEM((2 * tm, 2 * tm), BF16), pltpu.VMEM((N_EXPERTS, LANES), F32)],
        compiler_params=_cparams(("arbitrary",)),
        name="moe_router",
    )(h, w_rt, b_rt)


def _dispatch_kernel(pos_ref, prev_ref, h_hbm, xs_in, xs_out, hbuf, load_sem, row_sem):
    del xs_in
    i = pl.program_id(0)
    n = pl.num_programs(0)
    tm = pos_ref.shape[2]

    def tile_load(tile):
        s = tile % 2
        return pltpu.make_async_copy(h_hbm.at[pl.ds(tile * tm, tm), :], hbuf.at[s], load_sem.at[s])

    def row_copy(p_ref, tile, k, r):
        s = tile % 2
        return pltpu.make_async_copy(hbuf.at[s, pl.ds(r, 1), :], xs_out.at[pl.ds(p_ref[0, k, r], 1), :],
                                     row_sem.at[s])

    def wait_rows(p_ref, tile):
        def wait(r, carry):
            row_copy(p_ref, tile, 0, r).wait()
            row_copy(p_ref, tile, 1, r).wait()
            return carry
        lax.fori_loop(0, tm, wait, 0, unroll=DMA_UNROLL)

    @pl.when(i == 0)
    def _():
        tile_load(0).start()

    tile_load(i).wait()

    def start(r, carry):
        row_copy(pos_ref, i, 0, r).start()
        row_copy(pos_ref, i, 1, r).start()
        return carry

    lax.fori_loop(0, tm, start, 0, unroll=DMA_UNROLL)

    @pl.when(i > 0)
    def _():
        wait_rows(prev_ref, i - 1)

    @pl.when(i + 1 < n)
    def _():
        tile_load(i + 1).start()

    @pl.when(i == n - 1)
    def _():
        wait_rows(pos_ref, i)


def _dispatch(h, pos, slots, tm=ROW_TILE):
    t, d = h.shape
    return pl.pallas_call(
        _dispatch_kernel,
        grid=(t // tm,),
        in_specs=[pl.BlockSpec((1, 2, tm), lambda i: (i, 0, 0), memory_space=pltpu.SMEM),
                  pl.BlockSpec((1, 2, tm), lambda i: (jnp.maximum(i - 1, 0), 0, 0), memory_space=pltpu.SMEM),
                  pl.BlockSpec(memory_space=pl.ANY),
                  pl.BlockSpec(memory_space=pl.ANY)],
        out_specs=pl.BlockSpec(memory_space=pl.ANY),
        out_shape=jax.ShapeDtypeStruct(slots.shape, F32),
        input_output_aliases={3: 0},
        scratch_shapes=[pltpu.VMEM((2, tm, d), F32), pltpu.SemaphoreType.DMA((2,)), pltpu.SemaphoreType.DMA((2,))],
        compiler_params=_cparams(("arbitrary",)),
        name="moe_dispatch",
    )(pos, pos, h, slots)


def _moe_kernel(tile_e_ref, n_used_ref, x_ref, w1_ref, w3_ref, w2_ref, y_ref, w1b, w3b, w2b):
    i = pl.program_id(0)
    used = i < n_used_ref[0]
    new_expert = (i == 0) | (tile_e_ref[i] != tile_e_ref[jnp.maximum(i - 1, 0)])

    @pl.when(used & new_expert)
    def _():
        w1b[...] = w1_ref[0, 0].astype(BF16)
        w3b[...] = w3_ref[0, 0].astype(BF16)
        w2b[...] = w2_ref[0, 0].astype(BF16)

    @pl.when(used)
    def _():
        x = x_ref[...].astype(BF16)
        a = jnp.dot(x, w1b[...], preferred_element_type=F32)
        b = jnp.dot(x, w3b[...], preferred_element_type=F32)
        hid = _silu(a) * b
        y_ref[...] = jnp.dot(hid.astype(BF16), w2b[...], preferred_element_type=F32)

    @pl.when(i >= n_used_ref[0])
    def _():
        y_ref[...] = jnp.zeros_like(y_ref)


def _moe_experts(xs, tile_e, n_used, w1, w3, w2, layer, tm=MOE_TILE):
    n_slots, d = xs.shape
    ff = w1.shape[3]

    def row_map(i, te, nu):
        return (jnp.minimum(i, nu[0] - 1), 0)

    def w_map(i, te, nu):
        return (layer, te[i], 0, 0)

    grid_spec = pltpu.PrefetchScalarGridSpec(
        num_scalar_prefetch=2,
        grid=(n_slots // tm,),
        in_specs=[pl.BlockSpec((tm, d), row_map),
                  pl.BlockSpec((1, 1, d, ff), w_map),
                  pl.BlockSpec((1, 1, d, ff), w_map),
                  pl.BlockSpec((1, 1, ff, d), w_map)],
        out_specs=pl.BlockSpec((tm, d), lambda i, te, nu: (i, 0)),
        scratch_shapes=[pltpu.VMEM((d, ff), BF16), pltpu.VMEM((d, ff), BF16), pltpu.VMEM((ff, d), BF16)],
    )
    return pl.pallas_call(
        _moe_kernel,
        grid_spec=grid_spec,
        out_shape=jax.ShapeDtypeStruct((n_slots, d), F32),
        compiler_params=_cparams(("arbitrary",)),
        name="moe_experts",
    )(tile_e, n_used, xs, w1, w3, w2)


def _combine_kernel(final, pos_ref, next_ref, ys_hbm, wt_ref, x_ref, gate_ref, fg_ref, o_ref, buf, sem):
    i = pl.program_id(0)
    tm = x_ref.shape[0]
    slot = i % 2

    def row_copy(p_ref, s, k, r):
        return pltpu.make_async_copy(ys_hbm.at[pl.ds(p_ref[0, k, r], 1), :], buf.at[s, k, pl.ds(r, 1), :], sem.at[s])

    def start_all(p_ref, s):
        def start(r, carry):
            row_copy(p_ref, s, 0, r).start()
            row_copy(p_ref, s, 1, r).start()
            return carry
        lax.fori_loop(0, tm, start, 0, unroll=DMA_UNROLL)

    @pl.when(i == 0)
    def _():
        start_all(pos_ref, 0)

    @pl.when(i + 1 < pl.num_programs(0))
    def _():
        start_all(next_ref, 1 - slot)

    def wait(r, carry):
        row_copy(pos_ref, slot, 0, r).wait()
        row_copy(pos_ref, slot, 1, r).wait()
        return carry

    lax.fori_loop(0, tm, wait, 0, unroll=DMA_UNROLL)
    wt = wt_ref[...]
    x = x_ref[...] + gate_ref[0] * (wt[:, 0:1] * buf[slot, 0] + wt[:, 1:2] * buf[slot, 1])
    o_ref[...] = _rms(x, fg_ref[...]) if final else x


def _moe_combine(tok, ys, pos, wt_col, x, modr, layer, final_gain, final, tile0, n_tiles, tm=ROW_TILE):
    d = x.shape[1]
    base = (layer * 6 + 5) * MAX_CONDS
    return pl.pallas_call(
        functools.partial(_combine_kernel, final),
        grid=(n_tiles,),
        in_specs=[pl.BlockSpec((1, 2, tm), lambda i: (tile0 + i, 0, 0), memory_space=pltpu.SMEM),
                  pl.BlockSpec((1, 2, tm), lambda i: (tile0 + jnp.minimum(i + 1, n_tiles - 1), 0, 0),
                               memory_space=pltpu.SMEM),
                  pl.BlockSpec(memory_space=pl.ANY),
                  pl.BlockSpec((tm, 2), lambda i: (tile0 + i, 0)),
                  pl.BlockSpec((tm, d), lambda i: (tile0 + i, 0)),
                  pl.BlockSpec((1, 1, d), lambda i: (base + tok.cond_row(tile0 + i, tm), 0, 0)),
                  pl.BlockSpec((1, d), lambda i: (0, 0))],
        out_specs=pl.BlockSpec((tm, d), lambda i: (i, 0)),
        out_shape=jax.ShapeDtypeStruct((n_tiles * tm, d), F32),
        scratch_shapes=[pltpu.VMEM((2, 2, tm, d), F32), pltpu.SemaphoreType.DMA((2,))],
        compiler_params=_cparams(("arbitrary",)),
        name="moe_combine",
    )(pos, pos, ys, wt_col, x, modr, final_gain.reshape(1, d))


def _slot_tables(eid, rank, counts, tm):
    n_tiles = eid.size // tm + N_EXPERTS
    tiles_per_e = (counts + tm - 1) // tm
    tile_end = jnp.cumsum(tiles_per_e)
    slot0 = ((tile_end - tiles_per_e) * tm).astype(jnp.int32)
    n_used = tile_end[-1:].astype(jnp.int32)
    tile_e = jnp.sum(jnp.arange(n_tiles)[:, None] >= tile_end[None, :], axis=1)
    tile_e = jnp.minimum(tile_e, N_EXPERTS - 1).astype(jnp.int32)
    experts = jnp.arange(N_EXPERTS, dtype=jnp.int32)
    pos = rank + jnp.sum(jnp.where(eid[..., None] == experts, slot0, 0), axis=-1)
    return tile_e, n_used, pos


def _moe(tok, h, x, slots, w_rt, b_rt, w1, w3, w2, modr, layer, final_gain, final):
    eid, wt, rank, cnt = _router(h, w_rt, b_rt)
    tile_e, n_used, pos = _slot_tables(eid, rank, cnt[:, 0].astype(jnp.int32), MOE_TILE)
    xs = _dispatch(h, pos, slots)
    ys = _moe_experts(xs, tile_e, n_used, w1, w3, w2, layer)
    wt_col = wt.transpose(0, 2, 1).reshape(-1, 2)
    n_ctx_tiles = tok.t_ctx // ROW_TILE
    n_all_tiles = tok.t // ROW_TILE
    comb = functools.partial(_moe_combine, tok, ys, pos, wt_col, x, modr, layer, final_gain, final)
    if final:
        return (comb(0, n_ctx_tiles), comb(n_ctx_tiles, n_all_tiles - n_ctx_tiles)), xs
    return comb(0, n_all_tiles), xs


def _pick_tile(n, candidates):
    for c in candidates:
        if n % c == 0:
            return c
    raise ValueError(f"no tile for {n}")


def kernel(x_prompt, x_sample, cache_attn_k, cache_attn_v, state_ret, state_delta, c, c_ctx, w_mod, b_mod, norm1, norm2, w_in_even, diff_lambda, subln_gain, ret_decay, ret_norm, w_out_even, w_in_odd, conv_w, dn_a_log, dn_dt_bias, dn_norm, w_out_odd, moe_w_group, moe_b_group, moe_w_router, moe_b_router, moe_w1, moe_w3, moe_w2, final_norm):
    n_ctx, l_ctx, d = x_prompt.shape
    n_lat, l_lat, _ = x_sample.shape
    depth = w_mod.shape[0]
    tok = _Tokens(n_ctx, l_ctx, n_lat, l_lat)
    assert 1 + n_lat <= MAX_CONDS and l_ctx % ROW_TILE == 0 and l_lat % ROW_TILE == 0
    h_a = w_in_even.shape[2] // (7 * HEAD)
    h_c = dn_a_log.shape[2]
    d_c = h_c * HEAD

    x = jnp.concatenate([x_prompt.reshape(-1, d), x_sample.reshape(-1, d)], axis=0)
    cond = jnp.zeros((MAX_CONDS, d), F32).at[0].set(c_ctx).at[1:1 + n_lat].set(c)
    mod = _modulation(cond, w_mod, b_mod)
    modr = mod.reshape(depth, MAX_CONDS, 6, d).transpose(0, 2, 1, 3).reshape(depth * 6 * MAX_CONDS, 1, d)

    rt_rows = 4 * SUBLANES
    w_rt = jnp.concatenate([moe_w_group, moe_w_router], axis=2).transpose(0, 2, 1)
    w_rt = jnp.pad(w_rt, ((0, 0), (0, rt_rows - w_rt.shape[1]), (0, 0)))
    b_rt = jnp.pad(jnp.concatenate([moe_b_group, moe_b_router], axis=1),
                   ((0, 0), (0, rt_rows - N_GROUPS - N_EXPERTS)))[:, :, None]

    tm_in = _pick_tile(math.gcd(tok.t_ctx, l_lat), (1024, 512, ROW_TILE))
    even_caches = None
    delta_states = None
    slots = jnp.zeros(((2 * tok.t) // MOE_TILE + N_EXPERTS) * MOE_TILE * d, F32).reshape(-1, d)
    w_in_even_b = w_in_even.astype(BF16)
    w_in_odd_b = w_in_odd.astype(BF16)
    for layer in range(depth):
        i = layer // 2
        if layer % 2 == 0:
            n_even_cols = w_in_even.shape[2]
            p, = _in_proj(tok, x, norm1[layer], modr, layer, w_in_even_b, i, n_even_cols, tm_in,
                          _pick_tile(n_even_cols, (1024, 512)))
            prm = (diff_lambda[i], subln_gain[i], ret_norm[i], ret_decay[i])
            oa, orr, *even_caches = _even_ctx(tok, p, layer, i, *prm, even_caches)
            outs_ctx = [oa, orr]
            outs_lat = _even_lat(tok, p, layer, i, *prm, cache_attn_k, cache_attn_v, state_ret)
            w_out = w_out_even[i].astype(BF16)
            ws = [w_out[:h_a * HEAD], w_out[h_a * HEAD:]]
        else:
            n_main = 4 * d_c
            w_ab = jnp.pad(w_in_odd[i, :, n_main:], ((0, 0), (0, LANES - 4 * h_c))).astype(BF16)
            p, ab, abt = _in_proj(tok, x, norm1[layer], modr, layer, w_in_odd_b, i, n_main, tm_in,
                                  _pick_tile(n_main, (1024, 512)), w_ab)
            par = jnp.stack([dn_a_log[i], dn_dt_bias[i]])
            dargs = (tok, p, ab, abt, par, conv_w[i], dn_norm[i])
            o, delta_states = _delta(*dargs, True, i, delta_states)
            outs_ctx = [o]
            outs_lat = _delta(*dargs, False, i, state_delta)
            ws = [w_out_odd[i].astype(BF16)]
        x, h2 = _out_proj(tok, outs_ctx, outs_lat, ws, x, norm2[layer], modr, layer)
        final = layer == depth - 1
        x, slots = _moe(tok, h2, x, slots, w_rt[layer], b_rt[layer], moe_w1, moe_w3, moe_w2, modr, layer,
                        final_norm, final)

    y_ctx, y_lat = x
    return (y_ctx.reshape(n_ctx, l_ctx, d), y_lat.reshape(n_lat, l_lat, d), *even_caches, delta_states)
```

```python
import functools
import math

import jax
import jax.numpy as jnp
from jax import lax
from jax.experimental import pallas as pl
from jax.experimental.pallas import tpu as pltpu

F32 = jnp.float32
BF16 = jnp.bfloat16
HI = lax.Precision.HIGHEST

GRID_W = 64
DA = 64
HEAD = 128
CONV_K = 5
CHUNK = 64
ROPE_BASE = 10000.0
N_GROUPS = 4
E_PER_GROUP = 4
N_EXPERTS = N_GROUPS * E_PER_GROUP
EPS = 1e-6

LANES = 128
SUBLANES = 8
MAX_CONDS = SUBLANES
ROW_TILE = 256
MOE_TILE = 256
DMA_UNROLL = 8
VMEM_LIMIT = 56 * 1024 * 1024


def _cparams(sem):
    return pltpu.CompilerParams(dimension_semantics=sem, vmem_limit_bytes=VMEM_LIMIT)


def _bdot(a, b):
    return jnp.dot(a.astype(BF16), b.astype(BF16), preferred_element_type=F32)


def _bdot_nt(a, b):
    return lax.dot_general(a.astype(BF16), b.astype(BF16), (((1,), (1,)), ((), ())),
                           preferred_element_type=F32)


def _bdot_tn(a, b):
    return lax.dot_general(a.astype(BF16), b.astype(BF16), (((0,), (0,)), ((), ())),
                           preferred_element_type=F32)


def _hdot(a, b):
    return jnp.dot(a, b, precision=HI, preferred_element_type=F32)


def _idot(a, b):
    return _bdot(a, b)


def _sigmoid(x):
    return 1.0 / (1.0 + jnp.exp(-x))


def _silu(x):
    return x * _sigmoid(x)


def _softplus(x):
    return jnp.maximum(x, 0.0) + jnp.log(1.0 + jnp.exp(-jnp.abs(x)))


def _rms(x, gain):
    return x * lax.rsqrt(jnp.mean(x * x, axis=-1, keepdims=True) + EPS) * gain


def _softmax(s):
    m = jnp.max(s, axis=-1, keepdims=True)
    e = jnp.exp(s - m)
    return e / jnp.sum(e, axis=-1, keepdims=True)


def _mod_kernel(c_ref, w_ref, b_ref, o_ref):
    c = c_ref[...]
    o_ref[0] = _bdot(_silu(c), w_ref[0]) + b_ref[0]


def _modulation(cond, w_mod, b_mod, tn=1024):
    depth, d, n = w_mod.shape
    return pl.pallas_call(
        _mod_kernel,
        grid=(depth, n // tn),
        in_specs=[pl.BlockSpec((MAX_CONDS, d), lambda l, j: (0, 0)),
                  pl.BlockSpec((1, d, tn), lambda l, j: (l, 0, j)),
                  pl.BlockSpec((1, 1, tn), lambda l, j: (l, 0, j))],
        out_specs=pl.BlockSpec((1, MAX_CONDS, tn), lambda l, j: (l, 0, j)),
        out_shape=jax.ShapeDtypeStruct((depth, MAX_CONDS, n), F32),
        compiler_params=_cparams(("parallel", "parallel")),
        name="adaln_mod",
    )(cond, w_mod, b_mod.reshape(depth, 1, n))


class _Tokens:
    def __init__(self, n_ctx, l_ctx, n_lat, l_lat):
        self.n_ctx, self.l_ctx, self.n_lat, self.l_lat = n_ctx, l_ctx, n_lat, l_lat
        self.t_ctx = n_ctx * l_ctx
        self.t = self.t_ctx + n_lat * l_lat

    def cond_row(self, i, tm):
        n0 = self.t_ctx // tm
        return jnp.where(i < n0, 0, 1 + (i - n0) // (self.l_lat // tm))


def _mod_spec(tok, tm, layer, which, d):
    base = (layer * 6 + which) * MAX_CONDS
    return pl.BlockSpec((1, 1, d), lambda i, *_: (base + tok.cond_row(i, tm), 0, 0))


def _inproj_kernel(n_side, x_ref, g_ref, sh_ref, sc_ref, w_ref, *rest):
    side_w = rest[:2 * n_side]
    o_ref = rest[2 * n_side]
    side_o = rest[2 * n_side + 1:-1]
    h_scr = rest[-1]

    @pl.when(pl.program_id(1) == 0)
    def _():
        h = _rms(x_ref[...], g_ref[...]) * (1.0 + sc_ref[0]) + sh_ref[0]
        hb = h.astype(BF16)
        h_scr[...] = hb
        for s in range(n_side):
            side_o[2 * s][...] = jnp.dot(hb, side_w[2 * s][...], preferred_element_type=F32)
            side_o[2 * s + 1][...] = lax.dot_general(side_w[2 * s + 1][...], hb, (((1,), (1,)), ((), ())),
                                                     preferred_element_type=F32)

    o_ref[...] = jnp.dot(h_scr[...], w_ref[...], preferred_element_type=F32)


def _in_proj(tok, x, gain, modr, layer, w, w_idx, n, tm, tn, side_w=None):
    t, d = x.shape
    n_side = 0 if side_w is None else 1
    in_specs = [pl.BlockSpec((tm, d), lambda i, j: (i, 0)),
                pl.BlockSpec((1, d), lambda i, j: (0, 0)),
                _mod_spec(tok, tm, layer, 0, d),
                _mod_spec(tok, tm, layer, 1, d),
                pl.BlockSpec((None, d, tn), lambda i, j: (w_idx, 0, j))]
    args = [x, gain.reshape(1, d), modr, modr, w]
    out_specs = [pl.BlockSpec((tm, tn), lambda i, j: (i, j))]
    out_shape = [jax.ShapeDtypeStruct((t, n), F32)]
    if n_side:
        in_specs += [pl.BlockSpec((d, LANES), lambda i, j: (0, 0)), pl.BlockSpec((LANES, d), lambda i, j: (0, 0))]
        args += [side_w, side_w.T]
        out_specs += [pl.BlockSpec((tm, LANES), lambda i, j: (i, 0)), pl.BlockSpec((LANES, tm), lambda i, j: (0, i))]
        out_shape += [jax.ShapeDtypeStruct((t, LANES), F32), jax.ShapeDtypeStruct((LANES, t), F32)]
    return pl.pallas_call(
        functools.partial(_inproj_kernel, n_side),
        grid=(t // tm, n // tn),
        in_specs=in_specs,
        out_specs=out_specs,
        out_shape=out_shape,
        scratch_shapes=[pltpu.VMEM((tm, d), BF16)],
        compiler_params=_cparams(("parallel", "arbitrary")),
        name="in_proj",
    )(*args)


def _outproj_kernel(n_in, n_ctx_tiles, *refs):
    oc_refs = refs[:n_in]
    ol_refs = refs[n_in:2 * n_in]
    w_refs = refs[2 * n_in:3 * n_in]
    x_ref, gate_ref, g2_ref, sh_ref, sc_ref, xo_ref, h_ref = refs[3 * n_in:]
    is_ctx = pl.program_id(0) < n_ctx_tiles
    acc = None
    for oc, ol, w_r in zip(oc_refs, ol_refs, w_refs):
        part = _bdot(jnp.where(is_ctx, oc[...], ol[...]), w_r[...])
        acc = part if acc is None else acc + part
    x = x_ref[...] + gate_ref[0] * acc
    xo_ref[...] = x
    h = _rms(x, g2_ref[...]) * (1.0 + sc_ref[0]) + sh_ref[0]
    h_ref[...] = h


def _out_proj(tok, outs_ctx, outs_lat, ws, x, gain2, modr, layer, tm=ROW_TILE):
    t, d = x.shape
    n_in = len(ws)
    n0 = tok.t_ctx // tm
    in_specs = ([pl.BlockSpec((tm, o.shape[1]), lambda i: (jnp.minimum(i, n0 - 1), 0)) for o in outs_ctx]
                + [pl.BlockSpec((tm, o.shape[1]), lambda i: (jnp.maximum(i - n0, 0), 0)) for o in outs_lat]
                + [pl.BlockSpec(w.shape, lambda i: (0, 0)) for w in ws]
                + [pl.BlockSpec((tm, d), lambda i: (i, 0)),
                   _mod_spec(tok, tm, layer, 2, d),
                   pl.BlockSpec((1, d), lambda i: (0, 0)),
                   _mod_spec(tok, tm, layer, 3, d),
                   _mod_spec(tok, tm, layer, 4, d)])
    return pl.pallas_call(
        functools.partial(_outproj_kernel, n_in, n0),
        grid=(t // tm,),
        in_specs=in_specs,
        out_specs=[pl.BlockSpec((tm, d), lambda i: (i, 0)),
                   pl.BlockSpec((tm, d), lambda i: (i, 0))],
        out_shape=[jax.ShapeDtypeStruct((t, d), F32), jax.ShapeDtypeStruct((t, d), F32)],
        compiler_params=_cparams(("parallel",)),
        name="out_proj",
    )(*outs_ctx, *outs_lat, *ws, x, modr, gain2.reshape(1, d), modr, modr)


def _lambda(dl_ref, lam_init):
    dl = dl_ref[0]
    s1 = jnp.sum(dl[0:1] * dl[1:2], axis=-1, keepdims=True)
    s2 = jnp.sum(dl[2:3] * dl[3:4], axis=-1, keepdims=True)
    return jnp.exp(s1) - jnp.exp(s2) + lam_init


def _log_gammas(rdec_ref):
    lg = -_softplus(-rdec_ref[0])
    return lg[0:1, 0:1], lg[1:2, 0:1]


def _diff_attn(q, k1, k2, v, lam):
    p1 = _softmax(_bdot_nt(q[:, :DA], k1) * (DA ** -0.5))
    p2 = _softmax(_bdot_nt(q[:, DA:], k2) * (DA ** -0.5))
    return _bdot(p1 - lam * p2, v)


def _ret_mask(row0, nq, nk, lg_f, lg_b):
    qi = lax.broadcasted_iota(jnp.int32, (nq, nk), 0) + row0
    kj = lax.broadcasted_iota(jnp.int32, (nq, nk), 1)
    diff = (qi - kj).astype(F32)
    fwd = jnp.where(diff >= 0, jnp.exp(jnp.maximum(diff, 0.0) * lg_f), 0.0)
    bwd = jnp.where(diff <= 0, jnp.exp(jnp.maximum(-diff, 0.0) * lg_b), 0.0)
    return fwd + bwd


def _even_ctx_kernel(lam_init, n_prev, qa_ref, ka_ref, va_ref, qb_ref, kb_ref, vb_ref, gb_ref,
                     dl_ref, subln_ref, retn_ref, rdec_ref, *rest):
    if n_prev:
        kp_ref, vp_ref, sp_ref = rest[:3]
        rest = rest[3:]
    oa_ref, or_ref, ko_ref, vo_ref, so_ref = rest
    if n_prev:
        ko_ref[0, 0:n_prev] = kp_ref[0]
        vo_ref[0, 0:n_prev] = vp_ref[0]
        so_ref[0, 0:n_prev] = sp_ref[0]
    n = qa_ref.shape[0]
    heads = range(qa_ref.shape[1] // HEAD)
    lam = _lambda(dl_ref, lam_init)

    def cols(ref, hh):
        return ref[:, hh * HEAD:(hh + 1) * HEAD]

    ka = [cols(ka_ref, hh) for hh in heads]
    va = [cols(va_ref, hh) for hh in heads]
    for hh in heads:
        ko_ref[0, n_prev, hh] = ka[hh]
        vo_ref[0, n_prev, hh] = va[hh]
    kab = [x.astype(BF16) for x in ka]
    qa = [cols(qa_ref, hh) for hh in heads]
    s1 = [_bdot_nt(qa[hh][:, :DA], kab[hh][:, :DA]) * (DA ** -0.5) for hh in heads]
    s2 = [_bdot_nt(qa[hh][:, DA:], kab[hh][:, DA:]) * (DA ** -0.5) for hh in heads]

    kb = [cols(kb_ref, hh) * (HEAD ** -0.5) for hh in heads]
    vb = [cols(vb_ref, hh).astype(BF16) for hh in heads]
    sr = [_bdot_nt(cols(qb_ref, hh), kb[hh]) for hh in heads]

    att = [_softmax(s1[hh]) - lam * _softmax(s2[hh]) for hh in heads]
    oa = [_bdot(att[hh], va[hh].astype(BF16)) for hh in heads]
    lgs = [_log_gammas(rdec_ref.at[pl.ds(hh, 1)]) for hh in heads]
    ret = [sr[hh] * _ret_mask(0, n, n, *lgs[hh]) for hh in heads]
    orr = [_bdot(ret[hh], vb[hh]) for hh in heads]
    pos = lax.broadcasted_iota(jnp.int32, (n, 1), 0).astype(F32)
    sf = [_bdot_tn(kb[hh] * jnp.exp((n - 1.0 - pos) * lgs[hh][0]), vb[hh]) for hh in heads]
    sb = [_bdot_tn(kb[hh] * jnp.exp(pos * lgs[hh][1]), vb[hh]) for hh in heads]
    for hh in heads:
        lanes = slice(hh * HEAD, (hh + 1) * HEAD)
        oa_ref[:, lanes] = _rms(oa[hh], subln_ref[...]) * (1.0 - lam_init)
        or_ref[:, lanes] = _rms(orr[hh], retn_ref[...]) * _silu(cols(gb_ref, hh))
        so_ref[0, n_prev, 0, hh] = sf[hh]
        so_ref[0, n_prev, 1, hh] = sb[hh]


EVEN_HEADS = 8


def _head_spec(rows, col0, row0=0):
    return pl.BlockSpec((rows, HEAD), lambda s, h: (row0 + s, col0 + h))


def _even_param_specs():
    return [pl.BlockSpec((1, 4, DA), lambda s, h: (0, 0, 0)),
            pl.BlockSpec((1, HEAD), lambda s, h: (0, 0)),
            pl.BlockSpec((1, HEAD), lambda s, h: (0, 0)),
            pl.BlockSpec((1, 2, LANES), lambda s, h: (h, 0, 0))]


def _even_params(dl, subln, retn, rdec):
    n_heads = rdec.shape[1]
    rdec_b = jnp.broadcast_to(rdec.T[:, :, None], (n_heads, 2, LANES))
    return dl.reshape(1, 4, DA), subln.reshape(1, HEAD), retn.reshape(1, HEAD), rdec_b


def _even_ctx(tok, p, layer, i, dl, subln, retn, rdec, prev):
    n_heads = p.shape[1] // (7 * HEAD)
    lam_init = 0.8 - 0.6 * math.exp(-0.3 * layer)
    l, nseq = tok.l_ctx, tok.n_ctx
    dh = n_heads * HEAD
    hb = math.gcd(EVEN_HEADS, n_heads)
    groups = n_heads // hb
    args = [p] * 7 + list(_even_params(dl, subln, retn, rdec))
    in_specs = ([pl.BlockSpec((l, hb * HEAD), lambda s, g, c=c: (s, c * groups + g)) for c in range(7)]
                + _even_param_specs()[:3] + [pl.BlockSpec((hb, 2, LANES), lambda s, g: (g, 0, 0))])
    if i:
        in_specs += [pl.BlockSpec((1, i, hb, l, HEAD), lambda s, g: (s, 0, g, 0, 0)),
                     pl.BlockSpec((1, i, hb, l, HEAD), lambda s, g: (s, 0, g, 0, 0)),
                     pl.BlockSpec((1, i, 2, hb, HEAD, HEAD), lambda s, g: (s, 0, 0, g, 0, 0))]
        args += list(prev)
    return pl.pallas_call(
        functools.partial(_even_ctx_kernel, lam_init, i),
        grid=(nseq, groups),
        in_specs=in_specs,
        out_specs=[pl.BlockSpec((l, hb * HEAD), lambda s, g: (s, g)),
                   pl.BlockSpec((l, hb * HEAD), lambda s, g: (s, g)),
                   pl.BlockSpec((1, i + 1, hb, l, HEAD), lambda s, g: (s, 0, g, 0, 0)),
                   pl.BlockSpec((1, i + 1, hb, l, HEAD), lambda s, g: (s, 0, g, 0, 0)),
                   pl.BlockSpec((1, i + 1, 2, hb, HEAD, HEAD), lambda s, g: (s, 0, 0, g, 0, 0))],
        out_shape=[jax.ShapeDtypeStruct((tok.t_ctx, dh), F32),
                   jax.ShapeDtypeStruct((tok.t_ctx, dh), F32),
                   jax.ShapeDtypeStruct((nseq, i + 1, n_heads, l, HEAD), F32),
                   jax.ShapeDtypeStruct((nseq, i + 1, n_heads, l, HEAD), F32),
                   jax.ShapeDtypeStruct((nseq, i + 1, 2, n_heads, HEAD, HEAD), F32)],
        compiler_params=_cparams(("parallel", "parallel")),
        name="even_mixer_ctx",
    )(*args)


def _rope(x, cos, sin_signed):
    half = DA // 2
    lane = lax.broadcasted_iota(jnp.int32, x.shape, 1)
    swapped = jnp.where((lane % DA) < half,
                        pltpu.roll(x, x.shape[1] - half, axis=1),
                        pltpu.roll(x, half, axis=1))
    return x * cos + swapped * sin_signed


def _even_lat_kernel(lam_init, tq, qa_ref, ka_ref, va_ref, qb_ref, kb_ref, vb_ref, gb_ref,
                     ck_ref, cv_ref, s0_ref, cos_ref, sin_ref,
                     dl_ref, subln_ref, retn_ref, rdec_ref,
                     oa_ref, or_ref, kall, vall):
    n = qa_ref.shape[0]
    past = ck_ref.shape[3]
    lam = _lambda(dl_ref, lam_init)
    cos = cos_ref[...]
    sin = sin_ref[...]
    kall[0:past, :] = ck_ref[0, 0, 0].astype(BF16)
    kall[past:past + n, :] = _rope(ka_ref[...], cos, sin).astype(BF16)
    vall[0:past, :] = cv_ref[0, 0, 0].astype(BF16)
    vall[past:past + n, :] = va_ref[...].astype(BF16)

    lg_f, lg_b = _log_gammas(rdec_ref)
    kb = (kb_ref[...] * (HEAD ** -0.5)).astype(BF16)
    vb = vb_ref[...].astype(BF16)
    s0f = s0_ref[0, 0, 0, 0].astype(BF16)
    s0b = s0_ref[0, 0, 1, 0].astype(BF16)
    for blk in range(n // tq):
        rows = pl.ds(blk * tq, tq)
        q = _rope(qa_ref[rows, :], cos[blk * tq:(blk + 1) * tq], sin[blk * tq:(blk + 1) * tq])
        o = _diff_attn(q, kall[:, :DA], kall[:, DA:], vall[...], lam)
        oa_ref[rows, :] = _rms(o, subln_ref[...]) * (1.0 - lam_init)

        qb = qb_ref[rows, :]
        a = _bdot_nt(qb, kb) * _ret_mask(blk * tq, tq, n, lg_f, lg_b)
        pos = (lax.broadcasted_iota(jnp.int32, (tq, 1), 0) + blk * tq).astype(F32)
        o = (_bdot(a, vb) + _bdot(qb * jnp.exp((pos + 1.0) * lg_f), s0f)
             + _bdot(qb * jnp.exp((n - pos) * lg_b), s0b))
        or_ref[rows, :] = _rms(o, retn_ref[...]) * _silu(gb_ref[rows, :])


def _rope_tables(n_tok):
    n_freq = DA // 4
    t = jnp.arange(n_tok)
    inv_freq = ROPE_BASE ** (-jnp.arange(n_freq, dtype=F32) / n_freq)
    ang = jnp.concatenate([(t // GRID_W).astype(F32)[:, None] * inv_freq,
                           (t % GRID_W).astype(F32)[:, None] * inv_freq], axis=-1)
    cos, sin = jnp.cos(ang), jnp.sin(ang)
    return jnp.tile(cos, (1, 4)), jnp.tile(jnp.concatenate([-sin, sin], axis=-1), (1, 2))


def _even_lat(tok, p, layer, i, dl, subln, retn, rdec, cache_k, cache_v, state_ret, tq=256):
    n_heads = p.shape[1] // (7 * HEAD)
    lam_init = 0.8 - 0.6 * math.exp(-0.3 * layer)
    l, nseq = tok.l_lat, tok.n_lat
    row0 = tok.t_ctx // l
    past = cache_k.shape[3]
    dh = n_heads * HEAD
    cos, sin = _rope_tables(l)
    args = [p] * 7 + [cache_k, cache_v, state_ret, cos, sin] + list(_even_params(dl, subln, retn, rdec))
    return pl.pallas_call(
        functools.partial(_even_lat_kernel, lam_init, tq),
        grid=(nseq, n_heads),
        in_specs=[_head_spec(l, c * n_heads, row0) for c in range(7)]
        + [pl.BlockSpec((1, 1, 1, past, HEAD), lambda s, h: (s, i, h, 0, 0)),
           pl.BlockSpec((1, 1, 1, past, HEAD), lambda s, h: (s, i, h, 0, 0)),
           pl.BlockSpec((1, 1, 2, 1, HEAD, HEAD), lambda s, h: (s, i, 0, h, 0, 0)),
           pl.BlockSpec((l, HEAD), lambda s, h: (0, 0)),
           pl.BlockSpec((l, HEAD), lambda s, h: (0, 0))]
        + _even_param_specs(),
        out_specs=[pl.BlockSpec((l, HEAD), lambda s, h: (s, h)),
                   pl.BlockSpec((l, HEAD), lambda s, h: (s, h))],
        out_shape=[jax.ShapeDtypeStruct((nseq * l, dh), F32), jax.ShapeDtypeStruct((nseq * l, dh), F32)],
        scratch_shapes=[pltpu.VMEM((past + l, HEAD), BF16), pltpu.VMEM((past + l, HEAD), BF16)],
        compiler_params=_cparams(("parallel", "parallel")),
        name="even_mixer_lat",
    )(*args)


def _conv_silu(x, w):
    n = x.shape[0]
    pad = (CONV_K - 1) // 2
    row = lax.broadcasted_iota(jnp.int32, x.shape, 0)
    acc = x * w[pad:pad + 1]
    for j in range(CONV_K):
        d = j - pad
        if d == 0:
            continue
        shifted = pltpu.roll(x, (-d) % n, axis=0)
        valid = (row + d >= 0) & (row + d < n)
        acc = acc + jnp.where(valid, shifted, 0.0) * w[j:j + 1]
    return _silu(acc)


def _l2n(x):
    return x * lax.rsqrt(jnp.sum(x * x, axis=-1, keepdims=True) + EPS)


TRI_BLOCK = 16


def _unit_tri_solves(lms, rhss):
    c = lms[0].shape[0]
    ri = lax.broadcasted_iota(jnp.int32, (c, c), 0)
    ci = lax.broadcasted_iota(jnp.int32, (c, c), 1)
    blk = TRI_BLOCK
    same = (ri // blk) == (ci // blk)
    pws = [jnp.where(same, -lm, 0.0) for lm in lms]
    ns = list(pws)
    span = 1
    while 2 * span < blk:
        pws = [_idot(pw, pw) for pw in pws]
        ns = [n + pw + _idot(n, pw) for n, pw in zip(ns, pws)]
        span *= 2
    while blk < c:
        pair = ((ri // (2 * blk)) == (ci // (2 * blk))) & ((ri // blk) != (ci // blk))
        cs = [jnp.where(pair, lm, 0.0) for lm in lms]
        xs = [cm + _idot(n, cm) for n, cm in zip(ns, cs)]
        ns = [n - (x + _idot(x, n)) for n, x in zip(ns, xs)]
        blk *= 2
    return [rhs + _idot(n, rhs) for n, rhs in zip(ns, rhss)]


DELTA_UNIT = 4
DELTA_HEADS = 2
DELTA_HEADS_CTX = 4


def _delta_kernel(has_s0, n_prev, n_heads, q_ref, k_ref, v_ref, z_ref, wq_ref, wk_ref, wv_ref,
                  gc_ref, gr_ref, parr_ref, parc_ref, gain_ref, *rest):
    s0_ref = so_ref = None
    if has_s0:
        s0_ref, o_ref = rest[:2]
        rest = rest[2:]
    else:
        if n_prev:
            sp_ref, rest = rest[0], rest[1:]
        o_ref, so_ref = rest[:2]
        rest = rest[2:]
        if n_prev:
            so_ref[0, 0:n_prev] = sp_ref[0]
    qs, ks, vs, gcol, grow, gf_s, gt_s, b_s, w_s, q_s, o_s, e_s, st_s = rest
    n = q_ref.shape[0]
    n_chunks = n // CHUNK
    c = CHUNK
    grp = pl.program_id(1)
    heads = range(q_ref.shape[1] // HEAD)

    def cols(ref, hh):
        return ref[:, hh * HEAD:(hh + 1) * HEAD]

    for hh in heads:
        qs[hh] = _l2n(_conv_silu(cols(q_ref, hh), cols(wq_ref, hh))) * (HEAD ** -0.5)
        ks[hh] = _l2n(_conv_silu(cols(k_ref, hh), cols(wk_ref, hh)))
        vs[hh] = _conv_silu(cols(v_ref, hh), cols(wv_ref, hh))

    lane = lax.broadcasted_iota(jnp.int32, gc_ref.shape, 1)

    @pl.when(grp == 0)
    def _():
        xc = gc_ref[...]
        pr = parr_ref[...]
        gf_s[...] = jnp.where(lane < 2 * n_heads, -jnp.exp(pr[0:1]) * _softplus(xc + pr[1:2]), _sigmoid(xc))
        xr = gr_ref[...]
        sub = lax.broadcasted_iota(jnp.int32, xr.shape, 0)
        pc = parc_ref[...]
        gt_s[...] = jnp.where(sub < 2 * n_heads, -jnp.exp(pc[:, 0:1]) * _softplus(xr + pc[:, 1:2]), _sigmoid(xr))

    gfull = gf_s[...]
    for hh in heads:
        head = grp * len(heads) + hh
        chans = [jnp.sum(jnp.where(lane == r * n_heads + head, gfull, 0.0), axis=1, keepdims=True) for r in range(4)]
        gcol[hh] = jnp.concatenate(chans, axis=1)
        for r in range(4):
            row = gt_s[pl.ds(r * n_heads + head, 1), :]
            for j in range(n // LANES):
                grow[hh, j, r:r + 1, :] = row[:, j * LANES:(j + 1) * LANES]

    ri = lax.broadcasted_iota(jnp.int32, (c, c), 0)
    ci = lax.broadcasted_iota(jnp.int32, (c, c), 1)
    masks = [(ri >= ci, ri > ci), (ri <= ci, ri < ci)]

    unit = min(DELTA_UNIT, n_chunks)
    cpl = LANES // c
    assert unit % cpl == 0 and n_chunks % unit == 0

    def phase_a(hh, ui, carry):
        probs = []
        for cc in range(unit):
            cidx = ui * unit + cc
            rows = pl.ds(pl.multiple_of(cidx * c, c), c)
            qc, kc, vc = qs[hh, rows, :], ks[hh, rows, :], vs[hh, rows, :]
            kcb = kc.astype(BF16)
            kk = _bdot_nt(kcb, kcb)
            qkr = _bdot_nt(qc, kcb)
            gcl = gcol[hh, rows, :]
            grw = grow[hh, ui * (unit // cpl) + cc // cpl]
            lane0 = (cc % cpl) * c
            for d in range(2):
                incl, strict = masks[d]
                incl_t = masks[1 - d][0]
                g_c, b_c = gcl[:, d:d + 1], gcl[:, 2 + d:3 + d]
                g_r = grw[d:d + 1, lane0:lane0 + c]
                gcc = jnp.sum(jnp.where(incl, g_r, 0.0), axis=1, keepdims=True)
                gcr = jnp.sum(jnp.where(incl_t, g_c, 0.0), axis=0, keepdims=True)
                decay = jnp.where(incl, jnp.exp(jnp.where(incl, gcc - gcr, 0.0)), 0.0)
                egc = jnp.exp(gcc)
                g_last = jnp.sum(g_c, axis=0, keepdims=True)
                probs.append(dict(
                    d=d, cidx=cidx,
                    lm=jnp.where(strict, b_c * kk * decay, 0.0),
                    rhs=jnp.concatenate([vc * b_c, kc * (b_c * egc)], axis=-1),
                    qk=jnp.where(incl, qkr * decay, 0.0).astype(BF16),
                    kexp=(kc * jnp.exp(g_last - gcc)).astype(BF16),
                    qexp=qc * egc,
                    eg=jnp.broadcast_to(jnp.exp(g_last), (1, HEAD))))
        sols = _unit_tri_solves([p["lm"] for p in probs], [p["rhs"] for p in probs])
        sols = [s.astype(BF16) for s in sols]
        kts = [_bdot_tn(p["kexp"], s) for p, s in zip(probs, sols)]
        qos = [_bdot(p["qk"], s) for p, s in zip(probs, sols)]
        for p, kt, qo in zip(probs, kts, qos):
            d, cidx = p["d"], p["cidx"]
            b_s[hh, d, cidx] = kt[:, :HEAD]
            w_s[hh, d, cidx] = kt[:, HEAD:].astype(BF16)
            o_s[hh, d, cidx] = qo[:, :HEAD]
            q_s[hh, d, cidx] = (p["qexp"] - qo[:, HEAD:]).astype(BF16)
            e_s[hh, d, cidx] = p["eg"]
        return carry

    for hh in heads:
        lax.fori_loop(0, n_chunks // unit, functools.partial(phase_a, hh), 0)

    chains = [(hh, d) for hh in heads for d in range(2)]

    def phase_b(step, carry):
        nxt = []
        for (hh, d), s in zip(chains, carry):
            cidx = step if d == 0 else n_chunks - 1 - step
            sb = s.astype(BF16)
            st_s[hh, d, cidx] = sb
            nxt.append(e_s[hh, d, cidx] * s + b_s[hh, d, cidx]
                       - jnp.dot(w_s[hh, d, cidx], sb, preferred_element_type=F32))
        return tuple(nxt)

    if has_s0:
        s_init = tuple(s0_ref[0, 0, d, hh] for hh, d in chains)
    else:
        s_init = tuple(jnp.zeros((HEAD, HEAD), F32) for _ in chains)
    s_fin = lax.fori_loop(0, n_chunks, phase_b, s_init, unroll=n_chunks <= DELTA_UNIT)
    if so_ref is not None:
        for (hh, d), s in zip(chains, s_fin):
            so_ref[0, n_prev, d, hh] = s

    gain = gain_ref[...]

    def phase_c(hh, ui, carry):
        lanes = slice(hh * HEAD, (hh + 1) * HEAD)
        cids = [ui * unit + cc for cc in range(unit)]
        prods = [[jnp.dot(q_s[hh, d, cidx], st_s[hh, d, cidx], preferred_element_type=F32) for d in range(2)]
                 for cidx in cids]
        for cidx, (of, ob) in zip(cids, prods):
            rows = pl.ds(pl.multiple_of(cidx * c, c), c)
            o = (of + o_s[hh, 0, cidx]) + (ob + o_s[hh, 1, cidx])
            o_ref[rows, lanes] = _rms(o, gain) * _silu(z_ref[rows, lanes])
        return carry

    for hh in heads:
        lax.fori_loop(0, n_chunks // unit, functools.partial(phase_c, hh), 0)


def _delta(tok, p, ab, abt, par, conv_w, gain, ctx, j, state):
    n_heads = par.shape[-1]
    dc = n_heads * HEAD
    l, nseq = (tok.l_ctx, tok.n_ctx) if ctx else (tok.l_lat, tok.n_lat)
    row0 = 0 if ctx else tok.t_ctx // l
    nc = l // CHUNK
    flat = jnp.pad(jnp.concatenate([par[0].reshape(-1), par[1].reshape(-1)]).reshape(2, 2 * n_heads),
                   ((0, 0), (0, LANES - 2 * n_heads)))
    hb = math.gcd(DELTA_HEADS_CTX if ctx else DELTA_HEADS, n_heads)
    groups = n_heads // hb
    in_specs = ([pl.BlockSpec((l, hb * HEAD), lambda s, g, c=c: (row0 + s, c * groups + g)) for c in range(4)]
                + [pl.BlockSpec((CONV_K, hb * HEAD), lambda s, g, c=c: (0, c * groups + g)) for c in range(3)]
                + [pl.BlockSpec((l, LANES), lambda s, h: (row0 + s, 0)),
                   pl.BlockSpec((LANES, l), lambda s, h: (0, row0 + s)),
                   pl.BlockSpec((2, LANES), lambda s, h: (0, 0)),
                   pl.BlockSpec((LANES, 2), lambda s, h: (0, 0)),
                   pl.BlockSpec((1, HEAD), lambda s, h: (0, 0))])
    args = [p, p, p, p, conv_w, conv_w, conv_w, ab, abt, flat, flat.T, gain.reshape(1, HEAD)]
    out_specs = [pl.BlockSpec((l, hb * HEAD), lambda s, g: (s, g))]
    out_shape = [jax.ShapeDtypeStruct((nseq * l, dc), F32)]
    if ctx:
        out_specs.append(pl.BlockSpec((1, j + 1, 2, hb, HEAD, HEAD), lambda s, g: (s, 0, 0, g, 0, 0)))
        out_shape.append(jax.ShapeDtypeStruct((nseq, j + 1, 2, n_heads, HEAD, HEAD), F32))
        if j:
            in_specs.append(pl.BlockSpec((1, j, 2, hb, HEAD, HEAD), lambda s, g: (s, 0, 0, g, 0, 0)))
            args.append(state)
    else:
        in_specs.append(pl.BlockSpec((1, 1, 2, hb, HEAD, HEAD), lambda s, g: (s, j, 0, g, 0, 0)))
        args.append(state)
    return pl.pallas_call(
        functools.partial(_delta_kernel, not ctx, j if ctx else 0, n_heads),
        grid=(nseq, groups),
        in_specs=in_specs,
        out_specs=out_specs,
        out_shape=out_shape,
        scratch_shapes=[pltpu.VMEM((hb, l, HEAD), F32), pltpu.VMEM((hb, l, HEAD), F32), pltpu.VMEM((hb, l, HEAD), F32),
                        pltpu.VMEM((hb, l, 4), F32), pltpu.VMEM((hb, l // LANES, 4, LANES), F32),
                        pltpu.VMEM((l, LANES), F32), pltpu.VMEM((LANES, l), F32),
                        pltpu.VMEM((hb, 2, nc, HEAD, HEAD), F32), pltpu.VMEM((hb, 2, nc, HEAD, HEAD), BF16),
                        pltpu.VMEM((hb, 2, nc, CHUNK, HEAD), BF16), pltpu.VMEM((hb, 2, nc, CHUNK, HEAD), F32),
                        pltpu.VMEM((hb, 2, nc, 1, HEAD), F32), pltpu.VMEM((hb, 2, nc, HEAD, HEAD), BF16)],
        compiler_params=_cparams(("parallel", "arbitrary")),
        name="delta_ctx" if ctx else "delta_lat",
    )(*args)


def _router_kernel(h_ref, w_ref, b_ref, eid_ref, wt_ref, rank_ref, cnt_ref, tri_scr, run_scr):
    i = pl.program_id(0)
    tm = h_ref.shape[0]
    na = 2 * tm

    @pl.when(i == 0)
    def _():
        run_scr[...] = jnp.zeros_like(run_scr)
        a0 = lax.broadcasted_iota(jnp.int32, (na, na), 0)
        a1 = lax.broadcasted_iota(jnp.int32, (na, na), 1)
        tri_scr[...] = (a0 <= a1).astype(BF16)

    h = h_ref[...]
    w = w_ref[...]
    h_hi = h.astype(BF16)
    h_lo = (h - h_hi.astype(F32)).astype(BF16)
    w_hi = w.astype(BF16)
    w_lo = (w - w_hi.astype(F32)).astype(BF16)
    lt = _bdot_nt(w_hi, h_hi) + (_bdot_nt(w_lo, h_hi) + _bdot_nt(w_hi, h_lo)) + b_ref[...]

    def first_argmax(x, valid=None):
        rows = lax.broadcasted_iota(jnp.int32, x.shape, 0)
        if valid is not None:
            x = jnp.where(valid, x, -jnp.inf)
        m = jnp.max(x, axis=0, keepdims=True)
        idx = jnp.min(jnp.where(x == m, rows, x.shape[0]), axis=0, keepdims=True)
        return m, idx

    gl = lt[0:N_GROUPS]
    gmax, gidx = first_argmax(gl)
    g_w = 1.0 / jnp.sum(jnp.exp(gl - gmax), axis=0, keepdims=True)
    el = jnp.zeros((E_PER_GROUP, tm), F32)
    for g in range(N_GROUPS):
        lo = N_GROUPS + g * E_PER_GROUP
        el = jnp.where(gidx == g, lt[lo:lo + E_PER_GROUP], el)
    v1, i1 = first_argmax(el)
    rows = lax.broadcasted_iota(jnp.int32, el.shape, 0)
    v2, i2 = first_argmax(el, rows != i1)
    e21 = jnp.exp(v2 - v1)
    p1 = 1.0 / (1.0 + e21)
    e1 = gidx * E_PER_GROUP + i1
    e2 = gidx * E_PER_GROUP + i2
    eid_ref[...] = jnp.concatenate([e1, e2], axis=0)[None]
    wt_ref[...] = jnp.concatenate([g_w * p1, g_w * (e21 * p1)], axis=0)[None]

    e_all = jnp.concatenate([e1, e2], axis=1)
    onehot = lax.broadcasted_iota(jnp.int32, (N_EXPERTS, na), 0) == e_all
    csum = jnp.dot(onehot.astype(BF16), tri_scr[...], preferred_element_type=F32)
    run = run_scr[:, 0:1]
    rank = jnp.sum(jnp.where(onehot, csum + run, 0.0), axis=0, keepdims=True) - 1.0
    rank = rank.astype(jnp.int32)
    rank_ref[...] = jnp.concatenate([rank[:, :tm], rank[:, tm:]], axis=0)[None]
    run_new = run_scr[...] + csum[:, na - 1:na]
    run_scr[...] = run_new
    cnt_ref[...] = run_new


def _router(h, w_rt, b_rt, tm=ROW_TILE):
    t, d = h.shape
    nt = t // tm
    nr = w_rt.shape[0]
    tile_spec = pl.BlockSpec((1, 2, tm), lambda i: (i, 0, 0))
    return pl.pallas_call(
        _router_kernel,
        grid=(nt,),
        in_specs=[pl.BlockSpec((tm, d), lambda i: (i, 0)),
                  pl.BlockSpec((nr, d), lambda i: (0, 0)),
                  pl.BlockSpec((nr, 1), lambda i: (0, 0))],
        out_specs=[tile_spec, tile_spec, tile_spec,
                   pl.BlockSpec((N_EXPERTS, LANES), lambda i: (0, 0))],
        out_shape=[jax.ShapeDtypeStruct((nt, 2, tm), jnp.int32),
                   jax.ShapeDtypeStruct((nt, 2, tm), F32),
                   jax.ShapeDtypeStruct((nt, 2, tm), jnp.int32),
                   jax.ShapeDtypeStruct((N_EXPERTS, LANES), F32)],
        scratch_shapes=[pltpu.VMEM((2 * tm, 2 * tm), BF16), pltpu.VMEM((N_EXPERTS, LANES), F32)],
        compiler_params=_cparams(("arbitrary",)),
        name="moe_router",
    )(h, w_rt, b_rt)


def _dispatch_kernel(pos_ref, prev_ref, h_hbm, xs_in, xs_out, hbuf, load_sem, row_sem):
    del xs_in
    i = pl.program_id(0)
    n = pl.num_programs(0)
    tm = pos_ref.shape[2]

    def tile_load(tile):
        s = tile % 2
        return pltpu.make_async_copy(h_hbm.at[pl.ds(tile * tm, tm), :], hbuf.at[s], load_sem.at[s])

    def row_copy(p_ref, tile, k, r):
        s = tile % 2
        return pltpu.make_async_copy(hbuf.at[s, pl.ds(r, 1), :], xs_out.at[pl.ds(p_ref[0, k, r], 1), :],
                                     row_sem.at[s])

    def wait_rows(p_ref, tile):
        def wait(r, carry):
            row_copy(p_ref, tile, 0, r).wait()
            row_copy(p_ref, tile, 1, r).wait()
            return carry
        lax.fori_loop(0, tm, wait, 0, unroll=DMA_UNROLL)

    @pl.when(i == 0)
    def _():
        tile_load(0).start()

    tile_load(i).wait()

    def start(r, carry):
        row_copy(pos_ref, i, 0, r).start()
        row_copy(pos_ref, i, 1, r).start()
        return carry

    lax.fori_loop(0, tm, start, 0, unroll=DMA_UNROLL)

    @pl.when(i > 0)
    def _():
        wait_rows(prev_ref, i - 1)

    @pl.when(i + 1 < n)
    def _():
        tile_load(i + 1).start()

    @pl.when(i == n - 1)
    def _():
        wait_rows(pos_ref, i)


def _dispatch(h, pos, slots, tm=ROW_TILE):
    t, d = h.shape
    return pl.pallas_call(
        _dispatch_kernel,
        grid=(t // tm,),
        in_specs=[pl.BlockSpec((1, 2, tm), lambda i: (i, 0, 0), memory_space=pltpu.SMEM),
                  pl.BlockSpec((1, 2, tm), lambda i: (jnp.maximum(i - 1, 0), 0, 0), memory_space=pltpu.SMEM),
                  pl.BlockSpec(memory_space=pl.ANY),
                  pl.BlockSpec(memory_space=pl.ANY)],
        out_specs=pl.BlockSpec(memory_space=pl.ANY),
        out_shape=jax.ShapeDtypeStruct(slots.shape, F32),
        input_output_aliases={3: 0},
        scratch_shapes=[pltpu.VMEM((2, tm, d), F32), pltpu.SemaphoreType.DMA((2,)), pltpu.SemaphoreType.DMA((2,))],
        compiler_params=_cparams(("arbitrary",)),
        name="moe_dispatch",
    )(pos, pos, h, slots)


def _moe_kernel(tile_e_ref, n_used_ref, x_ref, w1_ref, w3_ref, w2_ref, y_ref, w1b, w3b, w2b):
    i = pl.program_id(0)
    used = i < n_used_ref[0]
    new_expert = (i == 0) | (tile_e_ref[i] != tile_e_ref[jnp.maximum(i - 1, 0)])

    @pl.when(used & new_expert)
    def _():
        w1b[...] = w1_ref[0, 0].astype(BF16)
        w3b[...] = w3_ref[0, 0].astype(BF16)
        w2b[...] = w2_ref[0, 0].astype(BF16)

    @pl.when(used)
    def _():
        x = x_ref[...].astype(BF16)
        a = jnp.dot(x, w1b[...], preferred_element_type=F32)
        b = jnp.dot(x, w3b[...], preferred_element_type=F32)
        hid = _silu(a) * b
        y_ref[...] = jnp.dot(hid.astype(BF16), w2b[...], preferred_element_type=F32)

    @pl.when(i >= n_used_ref[0])
    def _():
        y_ref[...] = jnp.zeros_like(y_ref)


def _moe_experts(xs, tile_e, n_used, w1, w3, w2, layer, tm=MOE_TILE):
    n_slots, d = xs.shape
    ff = w1.shape[3]

    def row_map(i, te, nu):
        return (jnp.minimum(i, nu[0] - 1), 0)

    def w_map(i, te, nu):
        return (layer, te[i], 0, 0)

    grid_spec = pltpu.PrefetchScalarGridSpec(
        num_scalar_prefetch=2,
        grid=(n_slots // tm,),
        in_specs=[pl.BlockSpec((tm, d), row_map),
                  pl.BlockSpec((1, 1, d, ff), w_map),
                  pl.BlockSpec((1, 1, d, ff), w_map),
                  pl.BlockSpec((1, 1, ff, d), w_map)],
        out_specs=pl.BlockSpec((tm, d), lambda i, te, nu: (i, 0)),
        scratch_shapes=[pltpu.VMEM((d, ff), BF16), pltpu.VMEM((d, ff), BF16), pltpu.VMEM((ff, d), BF16)],
    )
    return pl.pallas_call(
        _moe_kernel,
        grid_spec=grid_spec,
        out_shape=jax.ShapeDtypeStruct((n_slots, d), F32),
        compiler_params=_cparams(("arbitrary",)),
        name="moe_experts",
    )(tile_e, n_used, xs, w1, w3, w2)


def _combine_kernel(final, pos_ref, next_ref, ys_hbm, wt_ref, x_ref, gate_ref, fg_ref, o_ref, buf, sem):
    i = pl.program_id(0)
    tm = x_ref.shape[0]
    slot = i % 2

    def row_copy(p_ref, s, k, r):
        return pltpu.make_async_copy(ys_hbm.at[pl.ds(p_ref[0, k, r], 1), :], buf.at[s, k, pl.ds(r, 1), :], sem.at[s])

    def start_all(p_ref, s):
        def start(r, carry):
            row_copy(p_ref, s, 0, r).start()
            row_copy(p_ref, s, 1, r).start()
            return carry
        lax.fori_loop(0, tm, start, 0, unroll=DMA_UNROLL)

    @pl.when(i == 0)
    def _():
        start_all(pos_ref, 0)

    @pl.when(i + 1 < pl.num_programs(0))
    def _():
        start_all(next_ref, 1 - slot)

    def wait(r, carry):
        row_copy(pos_ref, slot, 0, r).wait()
        row_copy(pos_ref, slot, 1, r).wait()
        return carry

    lax.fori_loop(0, tm, wait, 0, unroll=DMA_UNROLL)
    wt = wt_ref[...]
    x = x_ref[...] + gate_ref[0] * (wt[:, 0:1] * buf[slot, 0] + wt[:, 1:2] * buf[slot, 1])
    o_ref[...] = _rms(x, fg_ref[...]) if final else x


def _moe_combine(tok, ys, pos, wt_col, x, modr, layer, final_gain, final, tile0, n_tiles, tm=ROW_TILE):
    d = x.shape[1]
    base = (layer * 6 + 5) * MAX_CONDS
    return pl.pallas_call(
        functools.partial(_combine_kernel, final),
        grid=(n_tiles,),
        in_specs=[pl.BlockSpec((1, 2, tm), lambda i: (tile0 + i, 0, 0), memory_space=pltpu.SMEM),
                  pl.BlockSpec((1, 2, tm), lambda i: (tile0 + jnp.minimum(i + 1, n_tiles - 1), 0, 0),
                               memory_space=pltpu.SMEM),
                  pl.BlockSpec(memory_space=pl.ANY),
                  pl.BlockSpec((tm, 2), lambda i: (tile0 + i, 0)),
                  pl.BlockSpec((tm, d), lambda i: (tile0 + i, 0)),
                  pl.BlockSpec((1, 1, d), lambda i: (base + tok.cond_row(tile0 + i, tm), 0, 0)),
                  pl.BlockSpec((1, d), lambda i: (0, 0))],
        out_specs=pl.BlockSpec((tm, d), lambda i: (i, 0)),
        out_shape=jax.ShapeDtypeStruct((n_tiles * tm, d), F32),
        scratch_shapes=[pltpu.VMEM((2, 2, tm, d), F32), pltpu.SemaphoreType.DMA((2,))],
        compiler_params=_cparams(("arbitrary",)),
        name="moe_combine",
    )(pos, pos, ys, wt_col, x, modr, final_gain.reshape(1, d))


def _slot_tables(eid, rank, counts, tm):
    n_tiles = eid.size // tm + N_EXPERTS
    tiles_per_e = (counts + tm - 1) // tm
    tile_end = jnp.cumsum(tiles_per_e)
    slot0 = ((tile_end - tiles_per_e) * tm).astype(jnp.int32)
    n_used = tile_end[-1:].astype(jnp.int32)
    tile_e = jnp.sum(jnp.arange(n_tiles)[:, None] >= tile_end[None, :], axis=1)
    tile_e = jnp.minimum(tile_e, N_EXPERTS - 1).astype(jnp.int32)
    experts = jnp.arange(N_EXPERTS, dtype=jnp.int32)
    pos = rank + jnp.sum(jnp.where(eid[..., None] == experts, slot0, 0), axis=-1)
    return tile_e, n_used, pos


def _moe(tok, h, x, slots, w_rt, b_rt, w1, w3, w2, modr, layer, final_gain, final):
    eid, wt, rank, cnt = _router(h, w_rt, b_rt)
    tile_e, n_used, pos = _slot_tables(eid, rank, cnt[:, 0].astype(jnp.int32), MOE_TILE)
    xs = _dispatch(h, pos, slots)
    ys = _moe_experts(xs, tile_e, n_used, w1, w3, w2, layer)
    wt_col = wt.transpose(0, 2, 1).reshape(-1, 2)
    n_ctx_tiles = tok.t_ctx // ROW_TILE
    n_all_tiles = tok.t // ROW_TILE
    comb = functools.partial(_moe_combine, tok, ys, pos, wt_col, x, modr, layer, final_gain, final)
    if final:
        return (comb(0, n_ctx_tiles), comb(n_ctx_tiles, n_all_tiles - n_ctx_tiles)), xs
    return comb(0, n_all_tiles), xs


def _pick_tile(n, candidates):
    for c in candidates:
        if n % c == 0:
            return c
    raise ValueError(f"no tile for {n}")


def kernel(x_prompt, x_sample, cache_attn_k, cache_attn_v, state_ret, state_delta, c, c_ctx, w_mod, b_mod, norm1, norm2, w_in_even, diff_lambda, subln_gain, ret_decay, ret_norm, w_out_even, w_in_odd, conv_w, dn_a_log, dn_dt_bias, dn_norm, w_out_odd, moe_w_group, moe_b_group, moe_w_router, moe_b_router, moe_w1, moe_w3, moe_w2, final_norm):
    n_ctx, l_ctx, d = x_prompt.shape
    n_lat, l_lat, _ = x_sample.shape
    depth = w_mod.shape[0]
    tok = _Tokens(n_ctx, l_ctx, n_lat, l_lat)
    assert 1 + n_lat <= MAX_CONDS and l_ctx % ROW_TILE == 0 and l_lat % ROW_TILE == 0
    h_a = w_in_even.shape[2] // (7 * HEAD)
    h_c = dn_a_log.shape[2]
    d_c = h_c * HEAD

    x = jnp.concatenate([x_prompt.reshape(-1, d), x_sample.reshape(-1, d)], axis=0)
    cond = jnp.zeros((MAX_CONDS, d), F32).at[0].set(c_ctx).at[1:1 + n_lat].set(c)
    mod = _modulation(cond, w_mod, b_mod)
    modr = mod.reshape(depth, MAX_CONDS, 6, d).transpose(0, 2, 1, 3).reshape(depth * 6 * MAX_CONDS, 1, d)

    rt_rows = 4 * SUBLANES
    w_rt = jnp.concatenate([moe_w_group, moe_w_router], axis=2).transpose(0, 2, 1)
    w_rt = jnp.pad(w_rt, ((0, 0), (0, rt_rows - w_rt.shape[1]), (0, 0)))
    b_rt = jnp.pad(jnp.concatenate([moe_b_group, moe_b_router], axis=1),
                   ((0, 0), (0, rt_rows - N_GROUPS - N_EXPERTS)))[:, :, None]

    tm_in = _pick_tile(math.gcd(tok.t_ctx, l_lat), (1024, 512, ROW_TILE))
    even_caches = None
    delta_states = None
    slots = jnp.zeros(((2 * tok.t) // MOE_TILE + N_EXPERTS) * MOE_TILE * d, F32).reshape(-1, d)
    w_in_even_b = w_in_even.astype(BF16)
    w_in_odd_b = w_in_odd.astype(BF16)
    for layer in range(depth):
        i = layer // 2
        if layer % 2 == 0:
            n_even_cols = w_in_even.shape[2]
            p, = _in_proj(tok, x, norm1[layer], modr, layer, w_in_even_b, i, n_even_cols, tm_in,
                          _pick_tile(n_even_cols, (1024, 512)))
            prm = (diff_lambda[i], subln_gain[i], ret_norm[i], ret_decay[i])
            oa, orr, *even_caches = _even_ctx(tok, p, layer, i, *prm, even_caches)
            outs_ctx = [oa, orr]
            outs_lat = _even_lat(tok, p, layer, i, *prm, cache_attn_k, cache_attn_v, state_ret)
            w_out = w_out_even[i].astype(BF16)
            ws = [w_out[:h_a * HEAD], w_out[h_a * HEAD:]]
        else:
            n_main = 4 * d_c
            w_ab = jnp.pad(w_in_odd[i, :, n_main:], ((0, 0), (0, LANES - 4 * h_c))).astype(BF16)
            p, ab, abt = _in_proj(tok, x, norm1[layer], modr, layer, w_in_odd_b, i, n_main, tm_in,
                                  _pick_tile(n_main, (1024, 512)), w_ab)
            par = jnp.stack([dn_a_log[i], dn_dt_bias[i]])
            dargs = (tok, p, ab, abt, par, conv_w[i], dn_norm[i])
            o, delta_states = _delta(*dargs, True, i, delta_states)
            outs_ctx = [o]
            outs_lat = _delta(*dargs, False, i, state_delta)
            ws = [w_out_odd[i].astype(BF16)]
        x, h2 = _out_proj(tok, outs_ctx, outs_lat, ws, x, norm2[layer], modr, layer)
        final = layer == depth - 1
        x, slots = _moe(tok, h2, x, slots, w_rt[layer], b_rt[layer], moe_w1, moe_w3, moe_w2, modr, layer,
                        final_norm, final)

    y_ctx, y_lat = x
    return (y_ctx.reshape(n_ctx, l_ctx, d), y_lat.reshape(n_lat, l_lat, d), *even_caches, delta_states)
```
